```python
import jax, jax.numpy as jnp
from jax import lax
import numpy as np

D_MODEL = 4096
BATCH = 4
SEQ = 4096
DEPTH = 4

ATTN_GROUPS = ((128, 1), (512, 4), (2048, 16))
N_GROUPS = 3
HEADS_PER_GROUP = 4
HEAD_DIM_A = 128
ATTN_WIDTH = N_GROUPS * HEADS_PER_GROUP * HEAD_DIM_A
ATTN_OUT_WIDTH = HEADS_PER_GROUP * HEAD_DIM_A
ROPE_DIM = HEAD_DIM_A // 4
ROPE_THETA = 500000.0
ML_HEADS = 8
ML_QK_DIM = 256
ML_V_DIM = 256
ML_QK_WIDTH = ML_HEADS * ML_QK_DIM
ML_V_WIDTH = ML_HEADS * ML_V_DIM
ML_CHUNK = 64
GATE_SOFTCAP = 15.0
N_EXPERTS = 16
CAPACITY_FACTOR = 2
D_EXPERT = 512
EPS = 1e-6
IN_WIDTHS = (ATTN_WIDTH, ATTN_WIDTH, ATTN_WIDTH, ML_QK_WIDTH, ML_QK_WIDTH, ML_V_WIDTH, ML_V_WIDTH, 4 * ML_HEADS, D_MODEL, D_MODEL)
D_IN = sum(IN_WIDTHS)

kernel_name = "hybrid_dilated_attn_bimlstm_ec_moe_encoder"


def rms_norm(x, gain):
    x32 = x.astype(jnp.float32)
    y = x32 * lax.rsqrt(jnp.mean(x32 * x32, axis=-1, keepdims=True) + EPS) * gain.astype(jnp.float32)
    return y.astype(x.dtype)


def rotary_tables(S):
    inv_freq = ROPE_THETA ** (-jnp.arange(0, ROPE_DIM, 2, dtype=jnp.float32) / ROPE_DIM)
    ang = jnp.arange(S, dtype=jnp.float32)[:, None] * inv_freq[None, :]
    return jnp.cos(ang), jnp.sin(ang)


def apply_partial_rope(t, cos, sin):
    half = ROPE_DIM // 2
    t32 = t.astype(jnp.float32)
    x1, x2 = t32[..., :half], t32[..., half:ROPE_DIM]
    c, s = cos[None, :, None, :], sin[None, :, None, :]
    return jnp.concatenate([x1 * c - x2 * s, x2 * c + x1 * s, t32[..., ROPE_DIM:]], axis=-1)


def dilated_window_attention(q, k, v, dilation, radius):
    B, S, H, Dh = q.shape
    n = S // dilation
    nb = -(-n // radius)
    n_pad = nb * radius

    def by_residue(t):
        return t.reshape(B, n, dilation, H, Dh).transpose(0, 2, 3, 1, 4)

    qr, kr, vr = by_residue(q), by_residue(k), by_residue(v)
    qb = jnp.pad(qr, [(0, 0)] * 3 + [(0, n_pad - n), (0, 0)]).reshape(B, dilation, H, nb, radius, Dh)

    def bands(t):
        tb = jnp.pad(t, [(0, 0)] * 3 + [(radius, radius + n_pad - n), (0, 0)])
        tb = tb.reshape(B, dilation, H, nb + 2, radius, Dh)
        return jnp.concatenate([tb[:, :, :, :-2], tb[:, :, :, 1:-1], tb[:, :, :, 2:]], axis=-2)

    kb, vb = bands(kr), bands(vr)
    blk = jnp.arange(nb)[:, None, None]
    q_idx = blk * radius + jnp.arange(radius)[None, :, None]
    k_idx = (blk - 1) * radius + jnp.arange(3 * radius)[None, None, :]
    mask = (jnp.abs(k_idx - q_idx) <= radius) & (k_idx >= 0) & (k_idx < n)
    s = jnp.einsum('bdhnqe,bdhnke->bdhnqk', qb, kb) * (Dh ** -0.5)
    s = jnp.where(mask, s, -jnp.inf)
    m = jnp.max(s, axis=-1, keepdims=True)
    p = jnp.exp(s - m)
    l = jnp.sum(p, axis=-1, keepdims=True)
    o = jnp.einsum('bdhnqk,bdhnke->bdhnqe', p, vb) / l
    lse = (m + jnp.log(l))[..., 0]
    o = o.reshape(B, dilation, H, n_pad, Dh)[:, :, :, :n].transpose(0, 3, 1, 2, 4).reshape(B, S, H, Dh)
    lse = lse.reshape(B, dilation, H, n_pad)[..., :n].transpose(0, 3, 1, 2).reshape(B, S, H)
    return o, lse


def mlstm_chunkwise(q, k, v, i_pre, f_pre):
    B, H, S, Dqk = q.shape
    Dv = v.shape[-1]
    L = ML_CHUNK
    nc = S // L

    def chunks(t):
        return jnp.moveaxis(t.reshape((B, H, nc, L) + t.shape[3:]), 2, 0)

    xs = (chunks(q), chunks(k * (Dqk ** -0.5)), chunks(v), chunks(i_pre), chunks(jax.nn.log_sigmoid(f_pre)))
    lower = jnp.tril(jnp.ones((L, L), dtype=bool))

    def step(carry, xc):
        C, nvec, m_prev = carry
        qc, kc, vc, ic, lfc = xc
        b = jnp.cumsum(lfc, axis=-1)
        log_d = jnp.where(lower, b[..., :, None] - b[..., None, :] + ic[..., None, :], -jnp.inf)
        log_inter = b + m_prev[..., None]
        m = jnp.maximum(log_inter, jnp.max(log_d, axis=-1))
        d = jnp.exp(log_d - m[..., None])
        w_inter = jnp.exp(log_inter - m)
        a = jnp.einsum('bhtd,bhsd->bhts', qc, kc) * d
        num = w_inter[..., None] * jnp.einsum('bhtd,bhde->bhte', qc, C) + jnp.einsum('bhts,bhse->bhte', a, vc)
        den = w_inter * jnp.einsum('bhtd,bhd->bht', qc, nvec) + jnp.sum(a, axis=-1)
        h = num / jnp.maximum(jnp.abs(den), jnp.exp(-m))[..., None]
        m_new = m[..., -1]
        w_s = jnp.exp(b[..., -1:] - b + ic - m_new[..., None])
        decay = jnp.exp(b[..., -1] + m_prev - m_new)
        C = decay[..., None, None] * C + jnp.einsum('bhs,bhsd,bhse->bhde', w_s, kc, vc)
        nvec = decay[..., None] * nvec + jnp.einsum('bhs,bhsd->bhd', w_s, kc)
        return (C, nvec, m_new), h

    init = (jnp.zeros((B, H, Dqk, Dv), jnp.float32), jnp.zeros((B, H, Dqk), jnp.float32),
            jnp.full((B, H), -jnp.inf, jnp.float32))
    _, h = lax.scan(step, init, xs)
    return jnp.moveaxis(h, 0, 2).reshape(B, H, S, Dv)


def hybrid_mixer(xn, w_in, ml_gate_bias, ml_norm_gain, w_attn_branch, w_mlstm_branch, w_out):
    B, S, _ = xn.shape
    splits = np.cumsum(IN_WIDTHS)[:-1].tolist()
    proj = xn @ w_in
    aq, ak, av, mq, mk, mv, mo, mgates, ga, gm = jnp.split(proj, splits, axis=-1)

    cos, sin = rotary_tables(S)
    n_attn = N_GROUPS * HEADS_PER_GROUP
    aq = apply_partial_rope(aq.reshape(B, S, n_attn, HEAD_DIM_A), cos, sin).reshape(B, S, N_GROUPS, HEADS_PER_GROUP, HEAD_DIM_A)
    ak = apply_partial_rope(ak.reshape(B, S, n_attn, HEAD_DIM_A), cos, sin).reshape(B, S, N_GROUPS, HEADS_PER_GROUP, HEAD_DIM_A)
    av = av.astype(jnp.float32).reshape(B, S, N_GROUPS, HEADS_PER_GROUP, HEAD_DIM_A)
    outs, lses = [], []
    for g, (window, dilation) in enumerate(ATTN_GROUPS):
        o, lse = dilated_window_attention(aq[:, :, g], ak[:, :, g], av[:, :, g], dilation, window // (2 * dilation))
        outs.append(o)
        lses.append(lse)
    wts = jax.nn.softmax(jnp.stack(lses, axis=0), axis=0)
    attn = jnp.sum(wts[..., None] * jnp.stack(outs, axis=0), axis=0)
    y_attn = attn.reshape(B, S, ATTN_OUT_WIDTH).astype(xn.dtype) @ w_attn_branch

    def heads(t, dh):
        return t.astype(jnp.float32).reshape(B, S, ML_HEADS, dh).transpose(0, 2, 1, 3)
    q, k, v = heads(mq, ML_QK_DIM), heads(mk, ML_QK_DIM), heads(mv, ML_V_DIM)
    gates = (mgates + ml_gate_bias).astype(jnp.float32)
    gates = GATE_SOFTCAP * jnp.tanh(gates / GATE_SOFTCAP)
    gates = gates.reshape(B, S, 4, ML_HEADS).transpose(2, 0, 3, 1)
    i_fw, f_fw, i_bw, f_bw = gates[0], gates[1], gates[2], gates[3]
    flip = lambda t: jnp.flip(t, axis=2)
    h_fw = mlstm_chunkwise(q, k, v, i_fw, f_fw)
    h_bw = flip(mlstm_chunkwise(flip(q), flip(k), flip(v), flip(i_bw), flip(f_bw)))
    h = (h_fw + h_bw).transpose(0, 2, 1, 3)
    h = h * lax.rsqrt(jnp.mean(h * h, axis=-1, keepdims=True) + EPS)
    h = h.reshape(B, S, ML_V_WIDTH) * ml_norm_gain.astype(jnp.float32) * jax.nn.sigmoid(mo.astype(jnp.float32))
    y_mlstm = h.astype(xn.dtype) @ w_mlstm_branch

    merged = jax.nn.sigmoid(ga) * y_attn + jax.nn.sigmoid(gm) * y_mlstm
    return merged @ w_out


def expert_choice_ffn(xn, w_router, w_gate, w_up, w_down):
    B, S, D = xn.shape
    cap = CAPACITY_FACTOR * S // N_EXPERTS
    affinity = jax.nn.softmax((xn @ w_router).astype(jnp.float32), axis=-1)
    g, idx = lax.top_k(jnp.swapaxes(affinity, 1, 2), cap)
    b_idx = jnp.arange(B)[:, None, None]
    xe = xn[b_idx, idx]
    h = jax.nn.silu(jnp.einsum('becd,edf->becf', xe, w_gate)) * jnp.einsum('becd,edf->becf', xe, w_up)
    ye = jnp.einsum('becf,efd->becd', h, w_down) * g[..., None].astype(xn.dtype)
    return jnp.zeros_like(xn).at[b_idx, idx].add(ye)


def setup_inputs(seed: int = 0) -> dict:
    key = jax.random.key(seed)
    ks = jax.random.split(key, 16)
    f32 = jnp.float32
    nrm = lambda k, shape, fan_in: jax.random.normal(k, shape, f32) * (fan_in ** -0.5)
    i_bias = 0.1 * jax.random.normal(ks[2], (DEPTH, 2, ML_HEADS), f32)
    f_bias = 3.0 + 3.0 * jax.random.uniform(ks[3], (DEPTH, 2, ML_HEADS), f32)
    ml_gate_bias = jnp.stack([i_bias[:, 0], f_bias[:, 0], i_bias[:, 1], f_bias[:, 1]], axis=1).reshape(DEPTH, 4 * ML_HEADS)
    return {
        "x": jax.random.normal(ks[0], (BATCH, SEQ, D_MODEL), f32),
        "w_in": nrm(ks[1], (DEPTH, D_MODEL, D_IN), D_MODEL),
        "ml_gate_bias": ml_gate_bias,
        "ml_norm_gain": 1.0 + 0.02 * jax.random.normal(ks[4], (DEPTH, ML_V_WIDTH), f32),
        "w_attn_branch": nrm(ks[5], (DEPTH, ATTN_OUT_WIDTH, D_MODEL), ATTN_OUT_WIDTH),
        "w_mlstm_branch": nrm(ks[6], (DEPTH, ML_V_WIDTH, D_MODEL), ML_V_WIDTH),
        "w_out": nrm(ks[7], (DEPTH, D_MODEL, D_MODEL), D_MODEL),
        "norm_mix_gain": 1.0 + 0.02 * jax.random.normal(ks[8], (DEPTH, D_MODEL), f32),
        "norm_ffn_gain": 1.0 + 0.02 * jax.random.normal(ks[9], (DEPTH, D_MODEL), f32),
        "w_router": nrm(ks[10], (DEPTH, D_MODEL, N_EXPERTS), D_MODEL),
        "w_expert_gate": nrm(ks[11], (DEPTH, N_EXPERTS, D_MODEL, D_EXPERT), D_MODEL),
        "w_expert_up": nrm(ks[12], (DEPTH, N_EXPERTS, D_MODEL, D_EXPERT), D_MODEL),
        "w_expert_down": nrm(ks[13], (DEPTH, N_EXPERTS, D_EXPERT, D_MODEL), D_EXPERT),
        "final_norm_gain": 1.0 + 0.02 * jax.random.normal(ks[14], (D_MODEL,), f32),
    }


def reference(x, w_in, ml_gate_bias, ml_norm_gain, w_attn_branch, w_mlstm_branch, w_out,
              norm_mix_gain, norm_ffn_gain, w_router, w_expert_gate, w_expert_up, w_expert_down,
              final_norm_gain):
    for layer in range(DEPTH):
        xn = rms_norm(x, norm_mix_gain[layer])
        x = x + hybrid_mixer(xn, w_in[layer], ml_gate_bias[layer], ml_norm_gain[layer],
                             w_attn_branch[layer], w_mlstm_branch[layer], w_out[layer])
        xn = rms_norm(x, norm_ffn_gain[layer])
        x = x + expert_choice_ffn(xn, w_router[layer], w_expert_gate[layer], w_expert_up[layer], w_expert_down[layer])
    return rms_norm(x, final_norm_gain)
```

```python
import functools

import jax
import jax.numpy as jnp
from jax import lax
from jax.experimental import pallas as pl
from jax.experimental.pallas import tpu as pltpu

F32 = jnp.float32
BF16 = jnp.bfloat16

ATTN_GROUPS = ((128, 1), (512, 4), (2048, 16))
HEADS_PER_GROUP = 4
HEAD_DIM_A = 128
N_ATTN_HEADS = len(ATTN_GROUPS) * HEADS_PER_GROUP
ATTN_WIDTH = N_ATTN_HEADS * HEAD_DIM_A
ATTN_OUT_WIDTH = HEADS_PER_GROUP * HEAD_DIM_A
ROPE_DIM = HEAD_DIM_A // 4
ROPE_HALF = ROPE_DIM // 2
ROPE_THETA = 500000.0
ML_HEADS = 8
ML_DIM = 256
ML_WIDTH = ML_HEADS * ML_DIM
ML_CHUNK = 64
GATE_SOFTCAP = 15.0
N_EXPERTS = 16
CAPACITY_FACTOR = 2
EPS = 1e-6
MAIN_WIDTH = 3 * ATTN_WIDTH + 4 * ML_WIDTH
N_GATES = 4 * ML_HEADS

LANES = 128
VMEM_LIMIT_BYTES = 56 * 1024 * 1024

NT_DIMS = (((1,), (1,)), ((), ()))
TN_DIMS = (((0,), (0,)), ((), ()))


def _params(*sem):
    return pltpu.CompilerParams(dimension_semantics=sem, vmem_limit_bytes=VMEM_LIMIT_BYTES)


def _rmsnorm_body(x_ref, g_ref, o_ref):
    x = x_ref[...]
    inv = lax.rsqrt(jnp.mean(x * x, axis=-1, keepdims=True) + EPS)
    o_ref[...] = (x * inv * g_ref[...]).astype(o_ref.dtype)


def rmsnorm(x2d, gain, out_dtype, tm=256):
    m, d = x2d.shape
    return pl.pallas_call(
        _rmsnorm_body,
        grid=(m // tm,),
        in_specs=[pl.BlockSpec((tm, d), lambda i: (i, 0)), pl.BlockSpec((1, d), lambda i: (0, 0))],
        out_specs=pl.BlockSpec((tm, d), lambda i: (i, 0)),
        out_shape=jax.ShapeDtypeStruct((m, d), out_dtype),
        compiler_params=_params("parallel"),
        name="rmsnorm",
    )(x2d, gain.reshape(1, d))


def _rmsnorm_router_body(x_ref, g_ref, wr_ref, xn_ref, aff_ref, *, n_experts):
    x = x_ref[...]
    inv = lax.rsqrt(jnp.mean(x * x, axis=-1, keepdims=True) + EPS)
    xn = x * inv * g_ref[...]
    xn_ref[...] = xn
    logits = jnp.dot(xn.astype(BF16), wr_ref[...], preferred_element_type=F32)
    lane = lax.broadcasted_iota(jnp.int32, logits.shape, 1)
    logits = jnp.where(lane < n_experts, logits, -jnp.inf)
    mx = jnp.max(logits, axis=-1, keepdims=True)
    p = jnp.exp(logits - mx)
    aff_ref[...] = p / jnp.sum(p, axis=-1, keepdims=True)


def rmsnorm_router(x2d, gain, w_router_pad, n_experts, tm=256):
    m, d = x2d.shape
    return pl.pallas_call(
        functools.partial(_rmsnorm_router_body, n_experts=n_experts),
        grid=(m // tm,),
        in_specs=[
            pl.BlockSpec((tm, d), lambda i: (i, 0)),
            pl.BlockSpec((1, d), lambda i: (0, 0)),
            pl.BlockSpec((d, LANES), lambda i: (0, 0)),
        ],
        out_specs=[pl.BlockSpec((tm, d), lambda i: (i, 0)), pl.BlockSpec((tm, LANES), lambda i: (i, 0))],
        out_shape=[jax.ShapeDtypeStruct((m, d), F32), jax.ShapeDtypeStruct((m, LANES), F32)],
        compiler_params=_params("parallel"),
        name="rmsnorm_router",
    )(x2d, gain.reshape(1, d), w_router_pad)


def _matmul_body(a_ref, w_ref, o_ref, *, act):
    acc = jnp.dot(a_ref[...], w_ref[...], preferred_element_type=F32)
    if act == "sigmoid":
        acc = jax.nn.sigmoid(acc)
    o_ref[...] = acc.astype(o_ref.dtype)


def matmul(a, w, out_dtype, tm, tn, act=None):
    m, k = a.shape
    n = w.shape[1]
    tm, tn = min(tm, m), min(tn, n)
    return pl.pallas_call(
        functools.partial(_matmul_body, act=act),
        grid=(n // tn, m // tm),
        in_specs=[pl.BlockSpec((tm, k), lambda j, i: (i, 0)), pl.BlockSpec((k, tn), lambda j, i: (0, j))],
        out_specs=pl.BlockSpec((tm, tn), lambda j, i: (i, j)),
        out_shape=jax.ShapeDtypeStruct((m, n), out_dtype),
        compiler_params=_params("parallel", "parallel"),
        name="proj_" + (act or "plain"),
    )(a, w)


def _gates_body(a_ref, w_ref, b_ref, o_ref):
    pre = jnp.dot(a_ref[...], w_ref[...], preferred_element_type=F32) + b_ref[...]
    o_ref[...] = GATE_SOFTCAP * jnp.tanh(pre / GATE_SOFTCAP)


def gates_proj(a, w_pad, bias_pad, tm=512):
    m, k = a.shape
    return pl.pallas_call(
        _gates_body,
        grid=(m // tm,),
        in_specs=[
            pl.BlockSpec((tm, k), lambda i: (i, 0)),
            pl.BlockSpec((k, LANES), lambda i: (0, 0)),
            pl.BlockSpec((1, LANES), lambda i: (0, 0)),
        ],
        out_specs=pl.BlockSpec((tm, LANES), lambda i: (i, 0)),
        out_shape=jax.ShapeDtypeStruct((m, LANES), F32),
        compiler_params=_params("parallel"),
        name="gates_proj",
    )(a, w_pad, bias_pad)


def _merge_branches_body(at_ref, hm_ref, sga_ref, sgm_ref, wa_ref, wm_ref, o_ref):
    ya = jnp.dot(at_ref[...], wa_ref[...], preferred_element_type=F32)
    ym = jnp.dot(hm_ref[...], wm_ref[...], preferred_element_type=F32)
    o_ref[...] = (sga_ref[...].astype(F32) * ya + sgm_ref[...].astype(F32) * ym).astype(o_ref.dtype)


def merge_branches(attn, hm, sig_gates, wa, wm, tm=512, tn=1024):
    m, ka = attn.shape
    km = hm.shape[1]
    n = wa.shape[1]
    tm, tn = min(tm, m), min(tn, n)
    nb = n // tn
    return pl.pallas_call(
        _merge_branches_body,
        grid=(nb, m // tm),
        in_specs=[
            pl.BlockSpec((tm, ka), lambda j, i: (i, 0)),
            pl.BlockSpec((tm, km), lambda j, i: (i, 0)),
            pl.BlockSpec((tm, tn), lambda j, i: (i, j)),
            pl.BlockSpec((tm, tn), lambda j, i: (i, j + nb)),
            pl.BlockSpec((ka, tn), lambda j, i: (0, j)),
            pl.BlockSpec((km, tn), lambda j, i: (0, j)),
        ],
        out_specs=pl.BlockSpec((tm, tn), lambda j, i: (i, j)),
        out_shape=jax.ShapeDtypeStruct((m, n), BF16),
        compiler_params=_params("parallel", "parallel"),
        name="merge_branches",
    )(attn, hm, sig_gates, sig_gates, wa, wm)


def _out_proj_body(a_ref, w_ref, r_ref, o_ref):
    o_ref[...] = r_ref[...] + jnp.dot(a_ref[...], w_ref[...], preferred_element_type=F32)


def out_proj_residual(a, w, resid, tm=512, tn=1024):
    m, k = a.shape
    n = w.shape[1]
    tm, tn = min(tm, m), min(tn, n)
    return pl.pallas_call(
        _out_proj_body,
        grid=(n // tn, m // tm),
        in_specs=[
            pl.BlockSpec((tm, k), lambda j, i: (i, 0)),
            pl.BlockSpec((k, tn), lambda j, i: (0, j)),
            pl.BlockSpec((tm, tn), lambda j, i: (i, j)),
        ],
        out_specs=pl.BlockSpec((tm, tn), lambda j, i: (i, j)),
        out_shape=jax.ShapeDtypeStruct((m, n), F32),
        compiler_params=_params("parallel", "parallel"),
        name="out_proj_residual",
    )(a, w, resid)


ATTN_TQ = 128
ROPE_ROWS = 256


def _attn_body(q_ref, k_ref, v_ref, cos_ref, sin_ref, o_ref, lse_ref, qs, ks, *, n, radius):
    tq = ATTN_TQ
    kw = tq + 2 * radius
    rb = min(ROPE_ROWS, n)
    lane = lax.broadcasted_iota(jnp.int32, (rb, LANES), 1)
    first_half = lane < ROPE_HALF

    def rope_rows(i, carry):
        r0 = pl.multiple_of(i * rb, rb)
        c = cos_ref[pl.ds(r0, rb), :]
        s = sin_ref[pl.ds(r0, rb), :]
        for src, dst, scale in ((q_ref, qs, HEAD_DIM_A ** -0.5), (k_ref, ks, 1.0)):
            t = src[0, pl.ds(r0, rb), :].astype(F32)
            partner = jnp.where(first_half, pltpu.roll(t, LANES - ROPE_HALF, 1), pltpu.roll(t, ROPE_HALF, 1))
            dst[pl.ds(r0, rb), :] = ((t * c + partner * s) * scale).astype(BF16)
        return carry

    lax.fori_loop(0, n // rb, rope_rows, 0)

    row = lax.broadcasted_iota(jnp.int32, (tq, kw), 0)
    col = lax.broadcasted_iota(jnp.int32, (tq, kw), 1)

    def q_block(i, carry):
        q0 = pl.multiple_of(i * tq, tq)
        ws = pl.multiple_of(jnp.clip(q0 - radius, 0, n - kw), radius)
        qb = qs[pl.ds(q0, tq), :]
        kb = ks[pl.ds(ws, kw), :]
        vb = v_ref[0, pl.ds(ws, kw), :]
        s = lax.dot_general(qb, kb, NT_DIMS, preferred_element_type=F32)
        dist = (col + ws) - (row + q0)
        s = jnp.where(jnp.abs(dist) <= radius, s, -jnp.inf)
        mx = jnp.max(s, axis=-1, keepdims=True)
        p = jnp.exp(s - mx)
        l = jnp.sum(p, axis=-1, keepdims=True)
        o = jnp.dot(p.astype(BF16), vb, preferred_element_type=F32) / l
        o_ref[0, pl.ds(q0, tq), :] = o
        lse_ref[0, pl.ds(q0, tq), :] = jnp.broadcast_to(mx + jnp.log(l), (tq, LANES))
        return carry

    lax.fori_loop(0, n // tq, q_block, 0)


def dilated_attention_group(proj, cos_t, sin_t, group, seq, proj_width):
    window, dil = ATTN_GROUPS[group]
    radius = window // (2 * dil)
    b = proj.shape[0]
    n = seq // dil
    hg = HEADS_PER_GROUP
    cols = proj_width // LANES
    pv = proj.reshape(b, n, dil * proj_width)
    cv = cos_t.reshape(n, dil * LANES)
    sv = sin_t.reshape(n, dil * LANES)
    head0 = group * hg

    def qmap(off):
        return lambda bi, h, r: (bi, 0, r * cols + off + head0 + h)

    out_sd = jax.ShapeDtypeStruct((b, n, dil * hg * LANES), F32)
    o, lse = pl.pallas_call(
        functools.partial(_attn_body, n=n, radius=radius),
        grid=(b, hg, dil),
        in_specs=[
            pl.BlockSpec((1, n, LANES), qmap(0)),
            pl.BlockSpec((1, n, LANES), qmap(N_ATTN_HEADS)),
            pl.BlockSpec((1, n, LANES), qmap(2 * N_ATTN_HEADS)),
            pl.BlockSpec((n, LANES), lambda bi, h, r: (0, r)),
            pl.BlockSpec((n, LANES), lambda bi, h, r: (0, r)),
        ],
        out_specs=[
            pl.BlockSpec((1, n, LANES), lambda bi, h, r: (bi, 0, r * hg + h)),
            pl.BlockSpec((1, n, LANES), lambda bi, h, r: (bi, 0, r * hg + h)),
        ],
        out_shape=[out_sd, out_sd],
        scratch_shapes=[pltpu.VMEM((n, LANES), BF16), pltpu.VMEM((n, LANES), BF16)],
        compiler_params=_params("parallel", "parallel", "parallel"),
        name=f"dilated_attn_g{group}",
    )(pv, pv, pv, cv, sv)
    width = hg * LANES
    return o.reshape(b * seq, width), lse.reshape(b * seq, width)


def _attn_merge_body(o0, o1, o2, l0, l1, l2, out_ref):
    la, lb, lc = l0[...], l1[...], l2[...]
    mx = jnp.maximum(jnp.maximum(la, lb), lc)
    wa, wb, wc = jnp.exp(la - mx), jnp.exp(lb - mx), jnp.exp(lc - mx)
    acc = wa * o0[...] + wb * o1[...] + wc * o2[...]
    out_ref[...] = (acc / (wa + wb + wc)).astype(out_ref.dtype)


def attn_merge(outs, lses, tm=1024):
    m, w = outs[0].shape
    spec = pl.BlockSpec((tm, w), lambda i: (i, 0))
    return pl.pallas_call(
        _attn_merge_body,
        grid=(m // tm,),
        in_specs=[spec] * 6,
        out_specs=spec,
        out_shape=jax.ShapeDtypeStruct((m, w), BF16),
        compiler_params=_params("parallel"),
        name="attn_merge",
    )(*outs, *lses)


def _gate_scan_body(g_ref, o_ref, *, rows):
    lane = lax.broadcasted_iota(jnp.int32, (rows, LANES), 1)
    pos = lane & (ML_CHUNK - 1)
    shifts = [1 << s for s in range(ML_CHUNK.bit_length() - 1)]

    def log_sigmoid(f):
        return jnp.minimum(f, 0.0) - jnp.log1p(jnp.exp(-jnp.abs(f)))

    def scan(y, combine, identity, reverse):
        for sh in shifts:
            if reverse:
                moved = jnp.where(pos < ML_CHUNK - sh, pltpu.roll(y, LANES - sh, 1), identity)
            else:
                moved = jnp.where(pos >= sh, pltpu.roll(y, sh, 1), identity)
            y = combine(y, moved)
        return y

    for d, reverse in ((0, False), (1, True)):
        i_pre = g_ref[0, pl.ds((2 * d) * rows, rows), :]
        f_pre = g_ref[0, pl.ds((2 * d + 1) * rows, rows), :]
        b = scan(log_sigmoid(f_pre), jnp.add, 0.0, reverse)
        w = i_pre - b
        pm = scan(w, jnp.maximum, -jnp.inf, reverse)
        o_ref[0, 3 * d + 0] = pm
        o_ref[0, 3 * d + 1] = b
        o_ref[0, 3 * d + 2] = w


def gate_scan(gates_t, rows):
    b = gates_t.shape[0]
    return pl.pallas_call(
        functools.partial(_gate_scan_body, rows=rows),
        grid=(b,),
        in_specs=[pl.BlockSpec((1, 4 * rows, LANES), lambda i: (i, 0, 0))],
        out_specs=pl.BlockSpec((1, 6, rows, LANES), lambda i: (i, 0, 0, 0)),
        out_shape=jax.ShapeDtypeStruct((b, 6, rows, LANES), F32),
        compiler_params=_params("parallel"),
        name="gate_scan",
    )(gates_t)


ML_COLS = 8
ML_NORM_ROWS = 256


def _mlstm_body(q_ref, k_ref, v_ref, mo_ref, col_ref, row_ref, gain_ref, o_ref,
                hf, hb, cf, cb, nf, nb, *, seq):
    L = ML_CHUNK
    nc = seq // L
    qk_scale = ML_DIM ** -0.5
    cf[...] = jnp.zeros_like(cf)
    cb[...] = jnp.zeros_like(cb)
    nf[...] = jnp.zeros_like(nf)
    nb[...] = jnp.zeros_like(nb)
    ti = lax.broadcasted_iota(jnp.int32, (L, L), 0)
    si = lax.broadcasted_iota(jnp.int32, (L, L), 1)

    def direction(c, m_prev, c_ref, n_ref, h_ref, col0, lane0, causal, last):
        r0 = pl.multiple_of(c * L, L)
        qc = q_ref[0, pl.ds(r0, L), :]
        kc = k_ref[0, pl.ds(r0, L), :]
        vc = v_ref[0, pl.ds(r0, L), :]
        cols = col_ref[0, 0, pl.ds(r0, L), :]
        pm = cols[:, col0:col0 + 1]
        bb = cols[:, col0 + 1:col0 + 2]
        wc = cols[:, col0 + 2:col0 + 3]
        w_row = row_ref[0, 0, pl.ds(c, 1), :][:, lane0:lane0 + L]
        b_tot = bb[last:last + 1, :]
        w_max = pm[last:last + 1, :]
        mt = jnp.maximum(m_prev, pm)
        dmat = jnp.where(causal, jnp.exp(w_row - mt), 0.0)
        s = lax.dot_general(qc, kc, NT_DIMS, preferred_element_type=F32) * qk_scale
        a = s * dmat
        intra = jnp.dot(a.astype(BF16), vc, preferred_element_type=F32)
        row_sum = jnp.sum(a, axis=1, keepdims=True)
        c_state = c_ref[...]
        inter = jnp.dot(qc, c_state.astype(BF16), preferred_element_type=F32)
        qn = jnp.sum(qc.astype(F32) * n_ref[...], axis=1, keepdims=True)
        w_inter = jnp.exp(m_prev - mt)
        num = w_inter * inter + intra
        den = w_inter * qn + row_sum
        h_ref[pl.ds(r0, L), :] = num / jnp.maximum(jnp.abs(den), jnp.exp(-(bb + mt)))
        mx = jnp.maximum(m_prev, w_max)
        w_s = jnp.exp(wc - mx) * qk_scale
        decay = jnp.exp(m_prev - mx)
        ks = kc.astype(F32) * w_s
        kv = lax.dot_general(ks.astype(BF16), vc, TN_DIMS, preferred_element_type=F32)
        c_ref[...] = decay * c_state + kv
        n_ref[...] = decay * n_ref[...] + jnp.sum(ks, axis=0, keepdims=True)
        return b_tot + mx

    def step(j, carry):
        m_f, m_b = carry
        m_f = direction(j, m_f, cf, nf, hf, 0, 0, ti >= si, L - 1)
        m_b = direction(nc - 1 - j, m_b, cb, nb, hb, 3, L, ti <= si, 0)
        return m_f, m_b

    m0 = jnp.full((1, 1), -jnp.inf, F32)
    lax.fori_loop(0, nc, step, (m0, m0))

    rb = min(ML_NORM_ROWS, seq)

    def finish(i, carry):
        r0 = pl.multiple_of(i * rb, rb)
        h = hf[pl.ds(r0, rb), :] + hb[pl.ds(r0, rb), :]
        h = h * lax.rsqrt(jnp.mean(h * h, axis=-1, keepdims=True) + EPS)
        og = jax.nn.sigmoid(mo_ref[0, pl.ds(r0, rb), :].astype(F32))
        o_ref[0, pl.ds(r0, rb), :] = (h * gain_ref[...] * og).astype(o_ref.dtype)
        return carry

    lax.fori_loop(0, seq // rb, finish, 0)


def mlstm(proj, col_params, row_params, gain, seq, q_col):
    b = proj.shape[0]
    nc = seq // ML_CHUNK

    def pmap(off):
        return lambda bi, h: (bi, 0, q_col + off * ML_HEADS + h)

    blk = (1, seq, ML_DIM)
    return pl.pallas_call(
        functools.partial(_mlstm_body, seq=seq),
        grid=(b, ML_HEADS),
        in_specs=[
            pl.BlockSpec(blk, pmap(0)),
            pl.BlockSpec(blk, pmap(1)),
            pl.BlockSpec(blk, pmap(2)),
            pl.BlockSpec(blk, pmap(3)),
            pl.BlockSpec((1, 1, seq, ML_COLS), lambda bi, h: (bi, h, 0, 0)),
            pl.BlockSpec((1, 1, nc, LANES), lambda bi, h: (bi, h, 0, 0)),
            pl.BlockSpec((1, ML_DIM), lambda bi, h: (0, h)),
        ],
        out_specs=pl.BlockSpec(blk, lambda bi, h: (bi, 0, h)),
        out_shape=jax.ShapeDtypeStruct((b, seq, ML_WIDTH), BF16),
        scratch_shapes=[
            pltpu.VMEM((seq, ML_DIM), F32),
            pltpu.VMEM((seq, ML_DIM), F32),
            pltpu.VMEM((ML_DIM, ML_DIM), F32),
            pltpu.VMEM((ML_DIM, ML_DIM), F32),
            pltpu.VMEM((1, ML_DIM), F32),
            pltpu.VMEM((1, ML_DIM), F32),
        ],
        compiler_params=_params("parallel", "parallel"),
        name="mlstm",
    )(proj, proj, proj, proj, col_params, row_params, gain.reshape(1, ML_WIDTH))


def _topk_body(aff_ref, idx_ref, gate_ref, pos_s, *, n_experts, rows, cap):
    a3 = aff_ref[0]
    bits3 = pltpu.bitcast(a3, jnp.int32)

    def count(mask3):
        c = jnp.sum(mask3.astype(jnp.int32), axis=2, keepdims=True)
        return jnp.sum(c, axis=1, keepdims=True)

    def search(i, t):
        cand = t | jnp.left_shift(jnp.int32(1), 30 - i)
        return jnp.where(count(bits3 >= cand) >= cap, cand, t)

    thr = lax.fori_loop(0, 31, search, jnp.zeros((n_experts, 1, 1), jnp.int32))
    need = (cap - count(bits3 > thr)).astype(F32)

    er = n_experts * rows
    shape3 = (n_experts, rows, LANES)
    bits2 = bits3.reshape(er, LANES)
    thr2 = jnp.broadcast_to(thr, shape3).reshape(er, LANES)
    need2 = jnp.broadcast_to(need, shape3).reshape(er, LANES)
    ri = lax.broadcasted_iota(jnp.int32, (LANES, LANES), 0)
    ci = lax.broadcasted_iota(jnp.int32, (LANES, LANES), 1)
    upper = jnp.where(ri <= ci, 1.0, 0.0).astype(BF16)
    gi = lax.broadcasted_iota(jnp.int32, (er, er), 0)
    gj = lax.broadcasted_iota(jnp.int32, (er, er), 1)
    same_expert = (gi - gj) < (rows - (gj & (rows - 1)))
    earlier_rows = jnp.where((gj < gi) & same_expert, 1.0, 0.0).astype(BF16)

    def exclusive_prefix(mask_f):
        incl = jnp.dot(mask_f.astype(BF16), upper, preferred_element_type=F32)
        row_tot = jnp.broadcast_to(incl[:, LANES - 1:LANES], (er, LANES)).astype(BF16)
        before = jnp.dot(earlier_rows, row_tot, preferred_element_type=F32)
        return incl + before - mask_f

    eq_f = jnp.where(bits2 == thr2, 1.0, 0.0)
    tie_ok = (bits2 == thr2) & (exclusive_prefix(eq_f) < need2)
    sel = (bits2 > thr2) | tie_ok
    sel_f = jnp.where(sel, 1.0, 0.0)
    pos_s[...] = jnp.where(sel, exclusive_prefix(sel_f), -1.0).reshape(shape3)

    slot = lax.broadcasted_iota(jnp.int32, (cap, LANES), 0).astype(F32)
    lane_f = lax.broadcasted_iota(jnp.int32, (1, LANES), 1).astype(F32)

    def per_expert(e, carry):
        def per_row(j, acc):
            acc_i, acc_g = acc
            p_row = pos_s[e, pl.ds(j, 1), :]
            a_row = aff_ref[0, e, pl.ds(j, 1), :]
            hit = p_row == slot
            tok = lane_f + lax.convert_element_type(j * LANES, F32)
            return acc_i + jnp.where(hit, tok, 0.0), acc_g + jnp.where(hit, a_row, 0.0)

        z = jnp.zeros((cap, LANES), F32)
        acc_i, acc_g = lax.fori_loop(0, rows, per_row, (z, z))
        idx_ref[0, e] = jnp.sum(acc_i, axis=1, keepdims=True).astype(jnp.int32)
        gate_ref[0, e] = jnp.sum(acc_g, axis=1, keepdims=True)
        return carry

    lax.fori_loop(0, n_experts, per_expert, 0)


def expert_topk(aff_t, cap):
    b, e, rows, _ = aff_t.shape
    return pl.pallas_call(
        functools.partial(_topk_body, n_experts=e, rows=rows, cap=cap),
        grid=(b,),
        in_specs=[pl.BlockSpec((1, e, rows, LANES), lambda i: (i, 0, 0, 0))],
        out_specs=[
            pl.BlockSpec((1, e, cap, 1), lambda i: (i, 0, 0, 0)),
            pl.BlockSpec((1, e, cap, 1), lambda i: (i, 0, 0, 0)),
        ],
        out_shape=[jax.ShapeDtypeStruct((b, e, cap, 1), jnp.int32), jax.ShapeDtypeStruct((b, e, cap, 1), F32)],
        scratch_shapes=[pltpu.VMEM((e, rows, LANES), F32)],
        compiler_params=_params("parallel"),
        name="expert_topk",
    )(aff_t)


DMA_UNROLL = 8
DOWN_COLS = 512


def _expert_ffn_body(idx_ref, gate_ref, xn_hbm, x_hbm, wg_ref, wu_ref, wd_ref, o_hbm,
                     xe, xr, sems, *, seq, cap):
    del x_hbm
    b = pl.program_id(1)
    base = b * seq

    def in_copies(c):
        row = base + idx_ref[0, 0, 0, c]
        return (
            pltpu.make_async_copy(xn_hbm.at[pl.ds(row, 1), :], xe.at[pl.ds(c, 1), :], sems.at[0]),
            pltpu.make_async_copy(o_hbm.at[pl.ds(row, 1), :], xr.at[pl.ds(c, 1), :], sems.at[1]),
        )

    def out_copy(c):
        row = base + idx_ref[0, 0, 0, c]
        return pltpu.make_async_copy(xr.at[pl.ds(c, 1), :], o_hbm.at[pl.ds(row, 1), :], sems.at[2])

    def start_in(c, carry):
        for cp in in_copies(c):
            cp.start()
        return carry

    def wait_in(c, carry):
        for cp in in_copies(c):
            cp.wait()
        return carry

    lax.fori_loop(0, cap, start_in, 0, unroll=DMA_UNROLL)
    lax.fori_loop(0, cap, wait_in, 0, unroll=DMA_UNROLL)

    x_in = xe[...].astype(BF16)
    hg = jnp.dot(x_in, wg_ref[0], preferred_element_type=F32)
    hu = jnp.dot(x_in, wu_ref[0], preferred_element_type=F32)
    hid = (jax.nn.silu(hg) * hu * gate_ref[0, 0]).astype(BF16)
    d = xr.shape[1]
    dc = min(DOWN_COLS, d)
    for j in range(d // dc):
        cs = slice(j * dc, (j + 1) * dc)
        xr[:, cs] += jnp.dot(hid, wd_ref[0, :, cs], preferred_element_type=F32)

    lax.fori_loop(0, cap, lambda c, carry: (out_copy(c).start(), carry)[1], 0, unroll=DMA_UNROLL)
    lax.fori_loop(0, cap, lambda c, carry: (out_copy(c).wait(), carry)[1], 0, unroll=DMA_UNROLL)


def expert_ffn(idx, gate, xn, x, wg, wu, wd, seq):
    bsz, n_e, cap, _ = gate.shape
    m, d = x.shape
    f = wg.shape[2]
    idx4 = idx.reshape(bsz, n_e, 1, cap)
    return pl.pallas_call(
        functools.partial(_expert_ffn_body, seq=seq, cap=cap),
        grid=(n_e, bsz),
        in_specs=[
            pl.BlockSpec((1, 1, 1, cap), lambda e, b: (b, e, 0, 0), memory_space=pltpu.SMEM),
            pl.BlockSpec((1, 1, cap, 1), lambda e, b: (b, e, 0, 0)),
            pl.BlockSpec(memory_space=pl.ANY),
            pl.BlockSpec(memory_space=pl.ANY),
            pl.BlockSpec((1, d, f), lambda e, b: (e, 0, 0)),
            pl.BlockSpec((1, d, f), lambda e, b: (e, 0, 0)),
            pl.BlockSpec((1, f, d), lambda e, b: (e, 0, 0)),
        ],
        out_specs=pl.BlockSpec(memory_space=pl.ANY),
        out_shape=jax.ShapeDtypeStruct((m, d), F32),
        scratch_shapes=[
            pltpu.VMEM((cap, d), F32),
            pltpu.VMEM((cap, d), F32),
            pltpu.SemaphoreType.DMA((3,)),
        ],
        input_output_aliases={3: 0},
        compiler_params=_params("arbitrary", "arbitrary"),
        name="expert_ffn",
    )(idx4, gate, xn, x, wg, wu, wd)


def _rope_tables(seq):
    inv_freq = ROPE_THETA ** (-jnp.arange(0, ROPE_DIM, 2, dtype=F32) / ROPE_DIM)
    ang = jnp.arange(seq, dtype=F32)[:, None] * inv_freq[None, :]
    cos, sin = jnp.cos(ang), jnp.sin(ang)
    rest = HEAD_DIM_A - ROPE_DIM
    cos_t = jnp.concatenate([cos, cos, jnp.ones((seq, rest), F32)], axis=1)
    sin_t = jnp.concatenate([-sin, sin, jnp.zeros((seq, rest), F32)], axis=1)
    return cos_t, sin_t


def _pad_cols(w, width):
    return jnp.pad(w, ((0, 0), (0, width - w.shape[1])))


def _mixer(x2d, xn, bsz, seq, w_in, gate_bias, ml_gain, w_attn, w_mlstm, w_out, cos_t, sin_t):
    m = bsz * seq
    d_model = x2d.shape[1]
    w_main = w_in[:, :MAIN_WIDTH].astype(BF16)
    w_gates = _pad_cols(w_in[:, MAIN_WIDTH:MAIN_WIDTH + N_GATES], LANES).astype(BF16)
    w_branch_gates = w_in[:, MAIN_WIDTH + N_GATES:].astype(BF16)
    bias_pad = _pad_cols(gate_bias.reshape(1, N_GATES), LANES)

    proj = matmul(xn, w_main, BF16, tm=512, tn=1280)
    gates = gates_proj(xn, w_gates, bias_pad)
    sig = matmul(xn, w_branch_gates, BF16, tm=512, tn=1024, act="sigmoid")

    proj3 = proj.reshape(bsz, seq, MAIN_WIDTH)
    outs, lses = [], []
    for g in range(len(ATTN_GROUPS)):
        o, lse = dilated_attention_group(proj3, cos_t, sin_t, g, seq, MAIN_WIDTH)
        outs.append(o)
        lses.append(lse)
    attn = attn_merge(outs, lses)

    rows = seq // LANES
    gates_t = gates[:, :N_GATES].reshape(bsz, seq, N_GATES).transpose(0, 2, 1).reshape(bsz, N_GATES * rows, LANES)
    scans = gate_scan(gates_t, ML_HEADS * rows).reshape(bsz, 6, ML_HEADS, seq)
    col_params = jnp.pad(scans.transpose(0, 2, 3, 1), ((0, 0), (0, 0), (0, 0), (0, ML_COLS - 6)))
    nc = seq // ML_CHUNK
    row_params = jnp.concatenate(
        [scans[:, 2].reshape(bsz, ML_HEADS, nc, ML_CHUNK), scans[:, 5].reshape(bsz, ML_HEADS, nc, ML_CHUNK)], axis=-1)
    hm = mlstm(proj3, col_params, row_params, ml_gain, seq, q_col=3 * ATTN_WIDTH // ML_DIM).reshape(m, ML_WIDTH)

    merged = merge_branches(attn, hm, sig, w_attn.astype(BF16), w_mlstm.astype(BF16))
    return out_proj_residual(merged, w_out.astype(BF16), x2d)


def _ffn(x2d, bsz, seq, gain, w_router, w_gate, w_up, w_down):
    n_e = w_router.shape[1]
    cap = CAPACITY_FACTOR * seq // n_e
    xn, aff = rmsnorm_router(x2d, gain, _pad_cols(w_router, LANES).astype(BF16), n_e)
    aff_t = aff[:, :n_e].reshape(bsz, seq, n_e).transpose(0, 2, 1).reshape(bsz, n_e, seq // LANES, LANES)
    idx, gate = expert_topk(aff_t, cap)
    return expert_ffn(idx, gate, xn, x2d, w_gate.astype(BF16), w_up.astype(BF16), w_down.astype(BF16), seq)


def kernel(x, w_in, ml_gate_bias, ml_norm_gain, w_attn_branch, w_mlstm_branch, w_out, norm_mix_gain,
           norm_ffn_gain, w_router, w_expert_gate, w_expert_up, w_expert_down, final_norm_gain):
    bsz, seq, d_model = x.shape
    depth = w_in.shape[0]
    cos_t, sin_t = _rope_tables(seq)
    x2d = x.reshape(bsz * seq, d_model)
    for layer in range(depth):
        xn = rmsnorm(x2d, norm_mix_gain[layer], BF16)
        x2d = _mixer(x2d, xn, bsz, seq, w_in[layer], ml_gate_bias[layer], ml_norm_gain[layer],
                     w_attn_branch[layer], w_mlstm_branch[layer], w_out[layer], cos_t, sin_t)
        x2d = _ffn(x2d, bsz, seq, norm_ffn_gain[layer], w_router[layer], w_expert_gate[layer],
                   w_expert_up[layer], w_expert_down[layer])
    return rmsnorm(x2d, final_norm_gain, F32).reshape(bsz, seq, d_model)
```

```python
import functools

import jax
import jax.numpy as jnp
from jax import lax
from jax.experimental import pallas as pl
from jax.experimental.pallas import tpu as pltpu

F32 = jnp.float32
BF16 = jnp.bfloat16

ATTN_GROUPS = ((128, 1), (512, 4), (2048, 16))
HEADS_PER_GROUP = 4
HEAD_DIM_A = 128
N_ATTN_HEADS = len(ATTN_GROUPS) * HEADS_PER_GROUP
ATTN_WIDTH = N_ATTN_HEADS * HEAD_DIM_A
ATTN_OUT_WIDTH = HEADS_PER_GROUP * HEAD_DIM_A
ROPE_DIM = HEAD_DIM_A // 4
ROPE_HALF = ROPE_DIM // 2
ROPE_THETA = 500000.0
ML_HEADS = 8
ML_DIM = 256
ML_WIDTH = ML_HEADS * ML_DIM
ML_CHUNK = 64
GATE_SOFTCAP = 15.0
N_EXPERTS = 16
CAPACITY_FACTOR = 2
EPS = 1e-6
MAIN_WIDTH = 3 * ATTN_WIDTH + 4 * ML_WIDTH
N_GATES = 4 * ML_HEADS

LANES = 128
VMEM_LIMIT_BYTES = 56 * 1024 * 1024

NT_DIMS = (((1,), (1,)), ((), ()))
TN_DIMS = (((0,), (0,)), ((), ()))


def _params(*sem):
    return pltpu.CompilerParams(dimension_semantics=sem, vmem_limit_bytes=VMEM_LIMIT_BYTES)


def _rmsnorm_body(x_ref, g_ref, o_ref):
    x = x_ref[...]
    inv = lax.rsqrt(jnp.mean(x * x, axis=-1, keepdims=True) + EPS)
    o_ref[...] = (x * inv * g_ref[...]).astype(o_ref.dtype)


def rmsnorm(x2d, gain, out_dtype, tm=256):
    m, d = x2d.shape
    return pl.pallas_call(
        _rmsnorm_body,
        grid=(m // tm,),
        in_specs=[pl.BlockSpec((tm, d), lambda i: (i, 0)), pl.BlockSpec((1, d), lambda i: (0, 0))],
        out_specs=pl.BlockSpec((tm, d), lambda i: (i, 0)),
        out_shape=jax.ShapeDtypeStruct((m, d), out_dtype),
        compiler_params=_params("parallel"),
        name="rmsnorm",
    )(x2d, gain.reshape(1, d))


def _rmsnorm_router_body(x_ref, g_ref, wr_ref, xn_ref, aff_ref, *, n_experts):
    x = x_ref[...]
    inv = lax.rsqrt(jnp.mean(x * x, axis=-1, keepdims=True) + EPS)
    xn = x * inv * g_ref[...]
    xn_ref[...] = xn
    logits = jnp.dot(xn.astype(BF16), wr_ref[...], preferred_element_type=F32)
    lane = lax.broadcasted_iota(jnp.int32, logits.shape, 1)
    logits = jnp.where(lane < n_experts, logits, -jnp.inf)
    mx = jnp.max(logits, axis=-1, keepdims=True)
    p = jnp.exp(logits - mx)
    aff_ref[...] = p / jnp.sum(p, axis=-1, keepdims=True)


def rmsnorm_router(x2d, gain, w_router_pad, n_experts, tm=256):
    m, d = x2d.shape
    return pl.pallas_call(
        functools.partial(_rmsnorm_router_body, n_experts=n_experts),
        grid=(m // tm,),
        in_specs=[
            pl.BlockSpec((tm, d), lambda i: (i, 0)),
            pl.BlockSpec((1, d), lambda i: (0, 0)),
            pl.BlockSpec((d, LANES), lambda i: (0, 0)),
        ],
        out_specs=[pl.BlockSpec((tm, d), lambda i: (i, 0)), pl.BlockSpec((tm, LANES), lambda i: (i, 0))],
        out_shape=[jax.ShapeDtypeStruct((m, d), F32), jax.ShapeDtypeStruct((m, LANES), F32)],
        compiler_params=_params("parallel"),
        name="rmsnorm_router",
    )(x2d, gain.reshape(1, d), w_router_pad)


def _matmul_body(a_ref, w_ref, o_ref, *, act):
    acc = jnp.dot(a_ref[...], w_ref[...], preferred_element_type=F32)
    if act == "sigmoid":
        acc = jax.nn.sigmoid(acc)
    o_ref[...] = acc.astype(o_ref.dtype)


def matmul(a, w, out_dtype, tm, tn, act=None):
    m, k = a.shape
    n = w.shape[1]
    tm, tn = min(tm, m), min(tn, n)
    return pl.pallas_call(
        functools.partial(_matmul_body, act=act),
        grid=(n // tn, m // tm),
        in_specs=[pl.BlockSpec((tm, k), lambda j, i: (i, 0)), pl.BlockSpec((k, tn), lambda j, i: (0, j))],
        out_specs=pl.BlockSpec((tm, tn), lambda j, i: (i, j)),
        out_shape=jax.ShapeDtypeStruct((m, n), out_dtype),
        compiler_params=_params("parallel", "parallel"),
        name="proj_" + (act or "plain"),
    )(a, w)


def _gates_body(a_ref, w_ref, b_ref, o_ref):
    pre = jnp.dot(a_ref[...], w_ref[...], preferred_element_type=F32) + b_ref[...]
    o_ref[...] = GATE_SOFTCAP * jnp.tanh(pre / GATE_SOFTCAP)


def gates_proj(a, w_pad, bias_pad, tm=512):
    m, k = a.shape
    return pl.pallas_call(
        _gates_body,
        grid=(m // tm,),
        in_specs=[
            pl.BlockSpec((tm, k), lambda i: (i, 0)),
            pl.BlockSpec((k, LANES), lambda i: (0, 0)),
            pl.BlockSpec((1, LANES), lambda i: (0, 0)),
        ],
        out_specs=pl.BlockSpec((tm, LANES), lambda i: (i, 0)),
        out_shape=jax.ShapeDtypeStruct((m, LANES), F32),
        compiler_params=_params("parallel"),
        name="gates_proj",
    )(a, w_pad, bias_pad)


def _merge_branches_body(at_ref, hm_ref, sga_ref, sgm_ref, wa_ref, wm_ref, o_ref):
    ya = jnp.dot(at_ref[...], wa_ref[...], preferred_element_type=F32)
    ym = jnp.dot(hm_ref[...], wm_ref[...], preferred_element_type=F32)
    o_ref[...] = (sga_ref[...].astype(F32) * ya + sgm_ref[...].astype(F32) * ym).astype(o_ref.dtype)


def merge_branches(attn, hm, sig_gates, wa, wm, tm=512, tn=1024):
    m, ka = attn.shape
    km = hm.shape[1]
    n = wa.shape[1]
    tm, tn = min(tm, m), min(tn, n)
    nb = n // tn
    return pl.pallas_call(
        _merge_branches_body,
        grid=(nb, m // tm),
        in_specs=[
            pl.BlockSpec((tm, ka), lambda j, i: (i, 0)),
            pl.BlockSpec((tm, km), lambda j, i: (i, 0)),
            pl.BlockSpec((tm, tn), lambda j, i: (i, j)),
            pl.BlockSpec((tm, tn), lambda j, i: (i, j + nb)),
            pl.BlockSpec((ka, tn), lambda j, i: (0, j)),
            pl.BlockSpec((km, tn), lambda j, i: (0, j)),
        ],
        out_specs=pl.BlockSpec((tm, tn), lambda j, i: (i, j)),
        out_shape=jax.ShapeDtypeStruct((m, n), BF16),
        compiler_params=_params("parallel", "parallel"),
        name="merge_branches",
    )(attn, hm, sig_gates, sig_gates, wa, wm)


def _out_proj_body(a_ref, w_ref, r_ref, o_ref):
    o_ref[...] = r_ref[...] + jnp.dot(a_ref[...], w_ref[...], preferred_element_type=F32)


def out_proj_residual(a, w, resid, tm=512, tn=1024):
    m, k = a.shape
    n = w.shape[1]
    tm, tn = min(tm, m), min(tn, n)
    return pl.pallas_call(
        _out_proj_body,
        grid=(n // tn, m // tm),
        in_specs=[
            pl.BlockSpec((tm, k), lambda j, i: (i, 0)),
            pl.BlockSpec((k, tn), lambda j, i: (0, j)),
            pl.BlockSpec((tm, tn), lambda j, i: (i, j)),
        ],
        out_specs=pl.BlockSpec((tm, tn), lambda j, i: (i, j)),
        out_shape=jax.ShapeDtypeStruct((m, n), F32),
        compiler_params=_params("parallel", "parallel"),
        name="out_proj_residual",
    )(a, w, resid)


def _qkv_rope_body(a_ref, w_ref, cos_ref, sin_ref, o_ref):
    acc = jnp.dot(a_ref[...], w_ref[...], preferred_element_type=F32)
    tm, tn = acc.shape
    c = cos_ref[0]
    s = sin_ref[0]
    first_half = lax.broadcasted_iota(jnp.int32, (tm, LANES), 1) < ROPE_HALF
    for h in range(tn // HEAD_DIM_A):
        hs = slice(h * HEAD_DIM_A, (h + 1) * HEAD_DIM_A)
        t = acc[:, hs]
        partner = jnp.where(first_half, pltpu.roll(t, LANES - ROPE_HALF, 1), pltpu.roll(t, ROPE_HALF, 1))
        o_ref[:, hs] = t * c + partner * s


def qkv_rope_proj(a, w, cos_tabs, sin_tabs, seq, tm=512):
    m, k = a.shape
    n = w.shape[1]
    tn = n // 3
    tm = min(tm, seq)
    per_seq = seq // tm
    return pl.pallas_call(
        _qkv_rope_body,
        grid=(3, m // tm),
        in_specs=[
            pl.BlockSpec((tm, k), lambda j, i: (i, 0)),
            pl.BlockSpec((k, tn), lambda j, i: (0, j)),
            pl.BlockSpec((1, tm, LANES), lambda j, i: (j, i % per_seq, 0)),
            pl.BlockSpec((1, tm, LANES), lambda j, i: (j, i % per_seq, 0)),
        ],
        out_specs=pl.BlockSpec((tm, tn), lambda j, i: (i, j)),
        out_shape=jax.ShapeDtypeStruct((m, n), F32),
        compiler_params=_params("parallel", "parallel"),
        name="qkv_rope_proj",
    )(a, w, cos_tabs, sin_tabs)


ATTN_TQ = 128
ATTN_LOAD_ROWS = 256


def _attn_body(q_ref, k_ref, v_ref, o_ref, lse_ref, qs, ks, vs, *, seq, dil, radius):
    n = seq // dil
    tq = ATTN_TQ
    kw = tq + 2 * radius
    rb = min(ATTN_LOAD_ROWS, n)
    row = lax.broadcasted_iota(jnp.int32, (tq, kw), 0)
    col = lax.broadcasted_iota(jnp.int32, (tq, kw), 1)

    def residue(r, carry):
        def regroup(i, c):
            a0 = pl.multiple_of(i * rb, rb)
            src = pl.ds(r + a0 * dil, rb, stride=dil)
            qs[pl.ds(a0, rb), :] = q_ref[0, src, :].astype(BF16)
            ks[pl.ds(a0, rb), :] = k_ref[0, src, :].astype(BF16)
            vs[pl.ds(a0, rb), :] = v_ref[0, src, :].astype(BF16)
            return c

        lax.fori_loop(0, n // rb, regroup, 0)

        def q_block(i, c):
            q0 = pl.multiple_of(i * tq, tq)
            ws = pl.multiple_of(jnp.clip(q0 - radius, 0, n - kw), radius)
            qb = qs[pl.ds(q0, tq), :]
            kb = ks[pl.ds(ws, kw), :]
            vb = vs[pl.ds(ws, kw), :]
            s = lax.dot_general(qb, kb, NT_DIMS, preferred_element_type=F32)
            dist = (col + ws) - (row + q0)
            s = jnp.where(jnp.abs(dist) <= radius, s, -jnp.inf)
            mx = jnp.max(s, axis=-1, keepdims=True)
            p = jnp.exp(s - mx)
            l = jnp.sum(p, axis=-1, keepdims=True)
            o = jnp.dot(p.astype(BF16), vb, preferred_element_type=F32) / l
            dst = pl.ds(r + q0 * dil, tq, stride=dil)
            o_ref[0, dst, :] = o
            lse_ref[0, dst, :] = jnp.broadcast_to(mx + jnp.log(l), (tq, LANES))
            return c

        lax.fori_loop(0, n // tq, q_block, 0, unroll=2)
        return carry

    lax.fori_loop(0, dil, residue, 0)


def dilated_attention_group(qkv, group, seq):
    window, dil = ATTN_GROUPS[group]
    radius = window // (2 * dil)
    b = qkv.shape[0]
    n = seq // dil
    hg = HEADS_PER_GROUP
    head0 = group * hg

    def qmap(off):
        return lambda bi, h: (bi, 0, off + head0 + h)

    blk = (1, seq, LANES)
    out_sd = jax.ShapeDtypeStruct((b, seq, hg * LANES), F32)
    o, lse = pl.pallas_call(
        functools.partial(_attn_body, seq=seq, dil=dil, radius=radius),
        grid=(b, hg),
        in_specs=[
            pl.BlockSpec(blk, qmap(0)),
            pl.BlockSpec(blk, qmap(N_ATTN_HEADS)),
            pl.BlockSpec(blk, qmap(2 * N_ATTN_HEADS)),
        ],
        out_specs=[pl.BlockSpec(blk, lambda bi, h: (bi, 0, h)), pl.BlockSpec(blk, lambda bi, h: (bi, 0, h))],
        out_shape=[out_sd, out_sd],
        scratch_shapes=[pltpu.VMEM((n, LANES), BF16)] * 3,
        compiler_params=_params("parallel", "parallel"),
        name=f"dilated_attn_g{group}",
    )(qkv, qkv, qkv)
    width = hg * LANES
    return o.reshape(b * seq, width), lse.reshape(b * seq, width)


def _attn_merge_body(o0, o1, o2, l0, l1, l2, out_ref):
    la, lb, lc = l0[...], l1[...], l2[...]
    mx = jnp.maximum(jnp.maximum(la, lb), lc)
    wa, wb, wc = jnp.exp(la - mx), jnp.exp(lb - mx), jnp.exp(lc - mx)
    acc = wa * o0[...] + wb * o1[...] + wc * o2[...]
    out_ref[...] = (acc / (wa + wb + wc)).astype(out_ref.dtype)


def attn_merge(outs, lses, tm=1024):
    m, w = outs[0].shape
    spec = pl.BlockSpec((tm, w), lambda i: (i, 0))
    return pl.pallas_call(
        _attn_merge_body,
        grid=(m // tm,),
        in_specs=[spec] * 6,
        out_specs=spec,
        out_shape=jax.ShapeDtypeStruct((m, w), BF16),
        compiler_params=_params("parallel"),
        name="attn_merge",
    )(*outs, *lses)


ML_SCAN_KINDS = 6
ML_BLOCK = LANES


def _gate_scan_body(g_ref, o_ref, tot, mst, *, rows, heads):
    rph = rows // heads
    lane = lax.broadcasted_iota(jnp.int32, (rows, LANES), 1)
    shifts = [1 << s for s in range(LANES.bit_length() - 1)]
    qk_scale = ML_DIM ** -0.5

    def log_sigmoid(f):
        return jnp.minimum(f, 0.0) - jnp.log1p(jnp.exp(-jnp.abs(f)))

    def scan(y, combine, identity, reverse):
        for sh in shifts:
            if reverse:
                moved = jnp.where(lane < LANES - sh, pltpu.roll(y, LANES - sh, 1), identity)
            else:
                moved = jnp.where(lane >= sh, pltpu.roll(y, sh, 1), identity)
            y = combine(y, moved)
        return y

    for d, reverse in ((0, False), (1, True)):
        i_pre = g_ref[0, pl.ds((2 * d) * rows, rows), :]
        f_pre = g_ref[0, pl.ds((2 * d + 1) * rows, rows), :]
        b = scan(log_sigmoid(f_pre), jnp.add, 0.0, reverse)
        w = i_pre - b
        pm = scan(w, jnp.maximum, -jnp.inf, reverse)
        last = 0 if reverse else LANES - 1
        tot[0] = jnp.broadcast_to(b[:, last:last + 1], (rows, LANES))
        tot[1] = jnp.broadcast_to(pm[:, last:last + 1], (rows, LANES))

        def step(t, m):
            idx = pl.ds((rph - 1 - t) if reverse else t, heads, stride=rph)
            mst[idx, :] = m
            return tot[0, idx, :] + jnp.maximum(m, tot[1, idx, :])

        lax.fori_loop(0, rph, step, jnp.full((heads, LANES), -jnp.inf, F32))
        m_in = mst[...]
        mt = jnp.maximum(m_in, pm)
        mx = jnp.maximum(m_in, tot[1])
        base = ML_SCAN_KINDS * d
        o_ref[0, base + 0] = mt
        o_ref[0, base + 1] = jnp.exp(m_in - mt)
        o_ref[0, base + 2] = jnp.exp(-(b + mt))
        o_ref[0, base + 3] = jnp.exp(m_in - mx)
        o_ref[0, base + 4] = jnp.exp(w - mx) * qk_scale
        o_ref[0, base + 5] = w


def gate_scan(gates_t, rows, heads):
    b = gates_t.shape[0]
    kinds = 2 * ML_SCAN_KINDS
    return pl.pallas_call(
        functools.partial(_gate_scan_body, rows=rows, heads=heads),
        grid=(b,),
        in_specs=[pl.BlockSpec((1, 4 * rows, LANES), lambda i: (i, 0, 0))],
        out_specs=pl.BlockSpec((1, kinds, rows, LANES), lambda i: (i, 0, 0, 0)),
        out_shape=jax.ShapeDtypeStruct((b, kinds, rows, LANES), F32),
        scratch_shapes=[pltpu.VMEM((2, rows, LANES), F32), pltpu.VMEM((rows, LANES), F32)],
        compiler_params=_params("parallel"),
        name="gate_scan",
    )(gates_t)


ML_PARAMS = 16
ML_N_ROWS = 8
ML_NORM_ROWS = 256


def _mlstm_body(q_ref, k_ref, v_ref, mo_ref, par_ref, gain_ref, o_ref, hf, hb, cf, cb, nf, nb, *, seq):
    L = ML_BLOCK
    nblk = seq // L
    cf[...] = jnp.zeros_like(cf)
    cb[...] = jnp.zeros_like(cb)
    nf[...] = jnp.zeros_like(nf)
    nb[...] = jnp.zeros_like(nb)
    ti = lax.broadcasted_iota(jnp.int32, (L, L), 0)
    si = lax.broadcasted_iota(jnp.int32, (L, L), 1)

    def block(r0, base, c_ref, n_ref, h_ref, causal):
        qc = q_ref[0, pl.ds(r0, L), :]
        kc = k_ref[0, pl.ds(r0, L), :]
        vc = v_ref[0, pl.ds(r0, L), :]
        par = par_ref[0, 0, :, pl.ds(r0, L)]
        cols = par.T
        mt, w_inter, e_negm = (cols[:, base + i:base + i + 1] for i in range(3))
        decay = par[base + 3:base + 4, 0:1]
        ws_row = par[base + 4:base + 5, :]
        w_row = par[base + 5:base + 6, :]
        dmat = jnp.where(causal, jnp.exp(w_row - mt), 0.0)
        s = lax.dot_general(qc, kc, NT_DIMS, preferred_element_type=F32) * (ML_DIM ** -0.5)
        a = s * dmat
        intra = jnp.dot(a.astype(BF16), vc, preferred_element_type=F32)
        row_sum = jnp.sum(a, axis=1, keepdims=True)
        inter = jnp.dot(qc, c_ref[...].astype(BF16), preferred_element_type=F32)
        qn = lax.dot_general(qc, n_ref[...].astype(BF16), NT_DIMS, preferred_element_type=F32)[:, 0:1]
        num = w_inter * inter + intra
        den = w_inter * qn + row_sum
        h_ref[pl.ds(r0, L), :] = num / jnp.maximum(jnp.abs(den), e_negm)
        ws_rows = jnp.broadcast_to(ws_row, (ML_N_ROWS, L)).astype(BF16)
        n_ref[...] = decay * n_ref[...] + jnp.dot(ws_rows, kc, preferred_element_type=F32)
        ks_t = (kc.T.astype(F32) * ws_row).astype(BF16)
        for half in range(ML_DIM // LANES):
            rs = slice(half * LANES, (half + 1) * LANES)
            kv = jnp.dot(ks_t[rs, :], vc, preferred_element_type=F32)
            c_ref[rs, :] = decay * c_ref[rs, :] + kv

    def step(j, carry):
        block(pl.multiple_of(j * L, L), 0, cf, nf, hf, ti >= si)
        block(pl.multiple_of((nblk - 1 - j) * L, L), ML_SCAN_KINDS, cb, nb, hb, ti <= si)
        return carry

    lax.fori_loop(0, nblk, step, 0)

    rb = min(ML_NORM_ROWS, seq)

    def finish(i, carry):
        r0 = pl.multiple_of(i * rb, rb)
        h = hf[pl.ds(r0, rb), :] + hb[pl.ds(r0, rb), :]
        h = h * lax.rsqrt(jnp.mean(h * h, axis=-1, keepdims=True) + EPS)
        og = jax.nn.sigmoid(mo_ref[0, pl.ds(r0, rb), :].astype(F32))
        o_ref[0, pl.ds(r0, rb), :] = (h * gain_ref[...] * og).astype(o_ref.dtype)
        return carry

    lax.fori_loop(0, seq // rb, finish, 0)


def mlstm(proj, params, gain, seq):
    b = proj.shape[0]

    def pmap(off):
        return lambda bi, h: (bi, 0, off * ML_HEADS + h)

    blk = (1, seq, ML_DIM)
    return pl.pallas_call(
        functools.partial(_mlstm_body, seq=seq),
        grid=(b, ML_HEADS),
        in_specs=[
            pl.BlockSpec(blk, pmap(0)),
            pl.BlockSpec(blk, pmap(1)),
            pl.BlockSpec(blk, pmap(2)),
            pl.BlockSpec(blk, pmap(3)),
            pl.BlockSpec((1, 1, ML_PARAMS, seq), lambda bi, h: (bi, h, 0, 0)),
            pl.BlockSpec((1, ML_DIM), lambda bi, h: (0, h)),
        ],
        out_specs=pl.BlockSpec(blk, lambda bi, h: (bi, 0, h)),
        out_shape=jax.ShapeDtypeStruct((b, seq, ML_WIDTH), BF16),
        scratch_shapes=[
            pltpu.VMEM((seq, ML_DIM), F32),
            pltpu.VMEM((seq, ML_DIM), F32),
            pltpu.VMEM((ML_DIM, ML_DIM), F32),
            pltpu.VMEM((ML_DIM, ML_DIM), F32),
            pltpu.VMEM((ML_N_ROWS, ML_DIM), F32),
            pltpu.VMEM((ML_N_ROWS, ML_DIM), F32),
        ],
        compiler_params=_params("parallel", "parallel"),
        name="mlstm",
    )(proj, proj, proj, proj, params, gain.reshape(1, ML_WIDTH))


TOPK_PARTS = 16


def _topk_body(aff_ref, idx_ref, gate_ref, pos_s, *, n_experts, rows, cap):
    a3 = aff_ref[0]
    bits3 = pltpu.bitcast(a3, jnp.int32)

    def count(mask3):
        c = jnp.sum(mask3.astype(jnp.int32), axis=2, keepdims=True)
        return jnp.sum(c, axis=1, keepdims=True)

    def search(i, t):
        cand = t | jnp.left_shift(jnp.int32(1), 30 - i)
        return jnp.where(count(bits3 >= cand) >= cap, cand, t)

    thr = lax.fori_loop(0, 31, search, jnp.zeros((n_experts, 1, 1), jnp.int32))
    need = (cap - count(bits3 > thr)).astype(F32)

    er = n_experts * rows
    shape3 = (n_experts, rows, LANES)
    bits2 = bits3.reshape(er, LANES)
    thr2 = jnp.broadcast_to(thr, shape3).reshape(er, LANES)
    need2 = jnp.broadcast_to(need, shape3).reshape(er, LANES)
    ri = lax.broadcasted_iota(jnp.int32, (LANES, LANES), 0)
    ci = lax.broadcasted_iota(jnp.int32, (LANES, LANES), 1)
    upper = jnp.where(ri <= ci, 1.0, 0.0).astype(BF16)
    gi = lax.broadcasted_iota(jnp.int32, (er, er), 0)
    gj = lax.broadcasted_iota(jnp.int32, (er, er), 1)
    same_expert = (gi - gj) < (rows - (gj & (rows - 1)))
    earlier_rows = jnp.where((gj < gi) & same_expert, 1.0, 0.0).astype(BF16)

    def exclusive_prefix(mask_f):
        incl = jnp.dot(mask_f.astype(BF16), upper, preferred_element_type=F32)
        row_tot = jnp.broadcast_to(incl[:, LANES - 1:LANES], (er, LANES)).astype(BF16)
        before = jnp.dot(earlier_rows, row_tot, preferred_element_type=F32)
        return incl + before - mask_f

    eq_f = jnp.where(bits2 == thr2, 1.0, 0.0)
    tie_ok = (bits2 == thr2) & (exclusive_prefix(eq_f) < need2)
    sel = (bits2 > thr2) | tie_ok
    sel_f = jnp.where(sel, 1.0, 0.0)
    pos_s[...] = jnp.where(sel, exclusive_prefix(sel_f), -1.0).reshape(shape3)

    slot = lax.broadcasted_iota(jnp.int32, (cap, LANES), 0).astype(F32)
    lane_i = lax.broadcasted_iota(jnp.int32, (1, LANES), 1)
    part = lax.broadcasted_iota(jnp.int32, (TOPK_PARTS, LANES), 0)

    def per_expert(e, carry):
        def per_row(j, acc):
            hit = jnp.where(pos_s[e, pl.ds(j, 1), :] == slot, 1.0, 0.0).astype(BF16)
            a_row = aff_ref[0, e, pl.ds(j, 1), :]
            tok = lane_i + j * LANES
            a1 = a_row.astype(BF16).astype(F32)
            r1 = a_row - a1
            a2 = r1.astype(BF16).astype(F32)
            parts = [(tok >> 6).astype(F32), (tok & 63).astype(F32), a1, a2, r1 - a2]
            lhs = jnp.zeros((TOPK_PARTS, LANES), F32)
            for p, v in enumerate(parts):
                lhs = jnp.where(part == p, v, lhs)
            return acc + lax.dot_general(lhs.astype(BF16), hit, NT_DIMS, preferred_element_type=F32)

        acc = lax.fori_loop(0, rows, per_row, jnp.zeros((TOPK_PARTS, cap), F32))
        idx_ref[0, e] = (acc[0:1] * 64.0 + acc[1:2]).astype(jnp.int32)
        gate_ref[0, e] = (acc[2:3] + acc[3:4]) + acc[4:5]
        return carry

    lax.fori_loop(0, n_experts, per_expert, 0)


def expert_topk(aff_t, cap):
    b, e, rows, _ = aff_t.shape
    assert rows & (rows - 1) == 0, "tokens per expert row group must be a power of two"
    return pl.pallas_call(
        functools.partial(_topk_body, n_experts=e, rows=rows, cap=cap),
        grid=(b,),
        in_specs=[pl.BlockSpec((1, e, rows, LANES), lambda i: (i, 0, 0, 0))],
        out_specs=[
            pl.BlockSpec((1, e, 1, cap), lambda i: (i, 0, 0, 0)),
            pl.BlockSpec((1, e, 1, cap), lambda i: (i, 0, 0, 0)),
        ],
        out_shape=[jax.ShapeDtypeStruct((b, e, 1, cap), jnp.int32), jax.ShapeDtypeStruct((b, e, 1, cap), F32)],
        scratch_shapes=[pltpu.VMEM((e, rows, LANES), F32)],
        compiler_params=_params("parallel"),
        name="expert_topk",
    )(aff_t)


DMA_UNROLL = 8
DOWN_COLS = 512


def _expert_ffn_body(idx_ref, gate_ref, xn_hbm, x_hbm, wg_ref, wu_ref, wd_ref, o_hbm,
                     xe, xr, sems, *, seq, cap):
    del x_hbm
    b = pl.program_id(1)
    base = b * seq

    def in_copies(c):
        row = base + idx_ref[0, 0, 0, c]
        return (
            pltpu.make_async_copy(xn_hbm.at[pl.ds(row, 1), :], xe.at[pl.ds(c, 1), :], sems.at[0]),
            pltpu.make_async_copy(o_hbm.at[pl.ds(row, 1), :], xr.at[pl.ds(c, 1), :], sems.at[1]),
        )

    def out_copy(c):
        row = base + idx_ref[0, 0, 0, c]
        return pltpu.make_async_copy(xr.at[pl.ds(c, 1), :], o_hbm.at[pl.ds(row, 1), :], sems.at[2])

    def start_in(c, carry):
        for cp in in_copies(c):
            cp.start()
        return carry

    def wait_in(c, carry):
        for cp in in_copies(c):
            cp.wait()
        return carry

    lax.fori_loop(0, cap, start_in, 0, unroll=DMA_UNROLL)
    lax.fori_loop(0, cap, wait_in, 0, unroll=DMA_UNROLL)

    x_in = xe[...].astype(BF16)
    hg = jnp.dot(x_in, wg_ref[0], preferred_element_type=F32)
    hu = jnp.dot(x_in, wu_ref[0], preferred_element_type=F32)
    hid = (jax.nn.silu(hg) * hu * gate_ref[0, 0]).astype(BF16)
    d = xr.shape[1]
    dc = min(DOWN_COLS, d)
    for j in range(d // dc):
        cs = slice(j * dc, (j + 1) * dc)
        xr[:, cs] += jnp.dot(hid, wd_ref[0, :, cs], preferred_element_type=F32)

    lax.fori_loop(0, cap, lambda c, carry: (out_copy(c).start(), carry)[1], 0, unroll=DMA_UNROLL)
    lax.fori_loop(0, cap, lambda c, carry: (out_copy(c).wait(), carry)[1], 0, unroll=DMA_UNROLL)


def expert_ffn(idx, gate, xn, x, wg, wu, wd, seq):
    bsz, n_e, cap, _ = gate.shape
    m, d = x.shape
    f = wg.shape[2]
    return pl.pallas_call(
        functools.partial(_expert_ffn_body, seq=seq, cap=cap),
        grid=(n_e, bsz),
        in_specs=[
            pl.BlockSpec((1, 1, 1, cap), lambda e, b: (b, e, 0, 0), memory_space=pltpu.SMEM),
            pl.BlockSpec((1, 1, cap, 1), lambda e, b: (b, e, 0, 0)),
            pl.BlockSpec(memory_space=pl.ANY),
            pl.BlockSpec(memory_space=pl.ANY),
            pl.BlockSpec((1, d, f), lambda e, b: (e, 0, 0)),
            pl.BlockSpec((1, d, f), lambda e, b: (e, 0, 0)),
            pl.BlockSpec((1, f, d), lambda e, b: (e, 0, 0)),
        ],
        out_specs=pl.BlockSpec(memory_space=pl.ANY),
        out_shape=jax.ShapeDtypeStruct((m, d), F32),
        scratch_shapes=[
            pltpu.VMEM((cap, d), F32),
            pltpu.VMEM((cap, d), F32),
            pltpu.SemaphoreType.DMA((3,)),
        ],
        input_output_aliases={3: 0},
        compiler_params=_params("arbitrary", "arbitrary"),
        name="expert_ffn",
    )(idx, gate, xn, x, wg, wu, wd)


def _rope_tables(seq):
    inv_freq = ROPE_THETA ** (-jnp.arange(0, ROPE_DIM, 2, dtype=F32) / ROPE_DIM)
    ang = jnp.arange(seq, dtype=F32)[:, None] * inv_freq[None, :]
    cos, sin = jnp.cos(ang), jnp.sin(ang)
    rest = HEAD_DIM_A - ROPE_DIM
    cos_t = jnp.concatenate([cos, cos, jnp.ones((seq, rest), F32)], axis=1)
    sin_t = jnp.concatenate([-sin, sin, jnp.zeros((seq, rest), F32)], axis=1)
    scale = HEAD_DIM_A ** -0.5
    cos_tabs = jnp.stack([cos_t * scale, cos_t, jnp.ones_like(cos_t)])
    sin_tabs = jnp.stack([sin_t * scale, sin_t, jnp.zeros_like(sin_t)])
    return cos_tabs, sin_tabs


def _pad_cols(w, width):
    return jnp.pad(w, ((0, 0), (0, width - w.shape[1])))


def _mixer(x2d, xn, bsz, seq, w_in, gate_bias, ml_gain, w_attn, w_mlstm, w_out, rope_tabs):
    m = bsz * seq
    qkv_w = 3 * ATTN_WIDTH
    w_qkv = w_in[:, :qkv_w].astype(BF16)
    w_ml = w_in[:, qkv_w:MAIN_WIDTH].astype(BF16)
    w_gates = _pad_cols(w_in[:, MAIN_WIDTH:MAIN_WIDTH + N_GATES], LANES).astype(BF16)
    w_branch_gates = w_in[:, MAIN_WIDTH + N_GATES:].astype(BF16)
    bias_pad = _pad_cols(gate_bias.reshape(1, N_GATES), LANES)

    qkv = qkv_rope_proj(xn, w_qkv, *rope_tabs, seq)
    ml_proj = matmul(xn, w_ml, BF16, tm=512, tn=2048)
    gates = gates_proj(xn, w_gates, bias_pad)
    sig = matmul(xn, w_branch_gates, BF16, tm=512, tn=2048, act="sigmoid")

    qkv3 = qkv.reshape(bsz, seq, qkv_w)
    outs, lses = [], []
    for g in range(len(ATTN_GROUPS)):
        o, lse = dilated_attention_group(qkv3, g, seq)
        outs.append(o)
        lses.append(lse)
    attn = attn_merge(outs, lses)

    rows = seq // LANES
    gates_t = gates[:, :N_GATES].reshape(bsz, seq, N_GATES).transpose(0, 2, 1).reshape(bsz, N_GATES * rows, LANES)
    scans = gate_scan(gates_t, ML_HEADS * rows, ML_HEADS)
    params = scans.reshape(bsz, 2 * ML_SCAN_KINDS, ML_HEADS, seq).transpose(0, 2, 1, 3)
    params = jnp.pad(params, ((0, 0), (0, 0), (0, ML_PARAMS - 2 * ML_SCAN_KINDS), (0, 0)))
    hm = mlstm(ml_proj.reshape(bsz, seq, 4 * ML_WIDTH), params, ml_gain, seq).reshape(m, ML_WIDTH)

    merged = merge_branches(attn, hm, sig, w_attn.astype(BF16), w_mlstm.astype(BF16))
    return out_proj_residual(merged, w_out.astype(BF16), x2d)


def _ffn(x2d, bsz, seq, gain, w_router, w_gate, w_up, w_down):
    n_e = w_router.shape[1]
    cap = CAPACITY_FACTOR * seq // n_e
    xn, aff = rmsnorm_router(x2d, gain, _pad_cols(w_router, LANES).astype(BF16), n_e)
    aff_t = aff[:, :n_e].reshape(bsz, seq, n_e).transpose(0, 2, 1).reshape(bsz, n_e, seq // LANES, LANES)
    idx, gate = expert_topk(aff_t, cap)
    return expert_ffn(idx, gate.reshape(bsz, n_e, cap, 1), xn, x2d,
                      w_gate.astype(BF16), w_up.astype(BF16), w_down.astype(BF16), seq)


def kernel(x, w_in, ml_gate_bias, ml_norm_gain, w_attn_branch, w_mlstm_branch, w_out, norm_mix_gain,
           norm_ffn_gain, w_router, w_expert_gate, w_expert_up, w_expert_down, final_norm_gain):
    bsz, seq, d_model = x.shape
    depth = w_in.shape[0]
    rope_tabs = _rope_tables(seq)
    x2d = x.reshape(bsz * seq, d_model)
    for layer in range(depth):
        xn = rmsnorm(x2d, norm_mix_gain[layer], BF16)
        x2d = _mixer(x2d, xn, bsz, seq, w_in[layer], ml_gate_bias[layer], ml_norm_gain[layer],
                     w_attn_branch[layer], w_mlstm_branch[layer], w_out[layer], rope_tabs)
        x2d = _ffn(x2d, bsz, seq, norm_ffn_gain[layer], w_router[layer], w_expert_gate[layer],
                   w_expert_up[layer], w_expert_down[layer])
    return rmsnorm(x2d, final_norm_gain, F32).reshape(bsz, seq, d_model)
```

```python
import functools

import jax
import jax.numpy as jnp
from jax import lax
from jax.experimental import pallas as pl
from jax.experimental.pallas import tpu as pltpu

F32 = jnp.float32
BF16 = jnp.bfloat16

ATTN_GROUPS = ((128, 1), (512, 4), (2048, 16))
HEADS_PER_GROUP = 4
HEAD_DIM_A = 128
N_ATTN_HEADS = len(ATTN_GROUPS) * HEADS_PER_GROUP
ATTN_WIDTH = N_ATTN_HEADS * HEAD_DIM_A
ATTN_OUT_WIDTH = HEADS_PER_GROUP * HEAD_DIM_A
ROPE_DIM = HEAD_DIM_A // 4
ROPE_HALF = ROPE_DIM // 2
ROPE_THETA = 500000.0
ML_HEADS = 8
ML_DIM = 256
ML_WIDTH = ML_HEADS * ML_DIM
ML_CHUNK = 64
GATE_SOFTCAP = 15.0
N_EXPERTS = 16
CAPACITY_FACTOR = 2
EPS = 1e-6
MAIN_WIDTH = 3 * ATTN_WIDTH + 4 * ML_WIDTH
N_GATES = 4 * ML_HEADS

LANES = 128
VMEM_LIMIT_BYTES = 60 * 1024 * 1024

NT_DIMS = (((1,), (1,)), ((), ()))
TN_DIMS = (((0,), (0,)), ((), ()))


def _params(*sem):
    return pltpu.CompilerParams(dimension_semantics=sem, vmem_limit_bytes=VMEM_LIMIT_BYTES)


def _rmsnorm_body(x_ref, g_ref, o_ref):
    x = x_ref[...]
    inv = lax.rsqrt(jnp.mean(x * x, axis=-1, keepdims=True) + EPS)
    o_ref[...] = (x * inv * g_ref[...]).astype(o_ref.dtype)


def rmsnorm(x2d, gain, out_dtype, tm=256):
    m, d = x2d.shape
    return pl.pallas_call(
        _rmsnorm_body,
        grid=(m // tm,),
        in_specs=[pl.BlockSpec((tm, d), lambda i: (i, 0)), pl.BlockSpec((1, d), lambda i: (0, 0))],
        out_specs=pl.BlockSpec((tm, d), lambda i: (i, 0)),
        out_shape=jax.ShapeDtypeStruct((m, d), out_dtype),
        compiler_params=_params("parallel"),
        name="rmsnorm",
    )(x2d, gain.reshape(1, d))


def _rmsnorm_router_body(x_ref, g_ref, wr_ref, xn_ref, aff_ref, *, n_experts):
    x = x_ref[...]
    inv = lax.rsqrt(jnp.mean(x * x, axis=-1, keepdims=True) + EPS)
    xn = x * inv * g_ref[...]
    xn_ref[...] = xn
    logits = jnp.dot(xn.astype(BF16), wr_ref[...], preferred_element_type=F32)
    lane = lax.broadcasted_iota(jnp.int32, logits.shape, 1)
    logits = jnp.where(lane < n_experts, logits, -jnp.inf)
    mx = jnp.max(logits, axis=-1, keepdims=True)
    p = jnp.exp(logits - mx)
    aff_ref[...] = p / jnp.sum(p, axis=-1, keepdims=True)


def rmsnorm_router(x2d, gain, w_router_pad, n_experts, tm=256):
    m, d = x2d.shape
    return pl.pallas_call(
        functools.partial(_rmsnorm_router_body, n_experts=n_experts),
        grid=(m // tm,),
        in_specs=[
            pl.BlockSpec((tm, d), lambda i: (i, 0)),
            pl.BlockSpec((1, d), lambda i: (0, 0)),
            pl.BlockSpec((d, LANES), lambda i: (0, 0)),
        ],
        out_specs=[pl.BlockSpec((tm, d), lambda i: (i, 0)), pl.BlockSpec((tm, LANES), lambda i: (i, 0))],
        out_shape=[jax.ShapeDtypeStruct((m, d), F32), jax.ShapeDtypeStruct((m, LANES), F32)],
        compiler_params=_params("parallel"),
        name="rmsnorm_router",
    )(x2d, gain.reshape(1, d), w_router_pad)


def _matmul_body(a_ref, w_ref, o_ref, *, act):
    acc = jnp.dot(a_ref[...], w_ref[...].astype(BF16), preferred_element_type=F32)
    if act == "sigmoid":
        acc = jax.nn.sigmoid(acc)
    o_ref[...] = acc.astype(o_ref.dtype)


def matmul(a, w_stack, layer, col0, n, out_dtype, tm, tn, act=None):
    m, k = a.shape
    tm, tn = min(tm, m), min(tn, n)
    assert col0 % tn == 0 and n % tn == 0
    return pl.pallas_call(
        functools.partial(_matmul_body, act=act),
        grid=(n // tn, m // tm),
        in_specs=[
            pl.BlockSpec((tm, k), lambda j, i: (i, 0)),
            pl.BlockSpec((None, k, tn), lambda j, i: (layer, 0, col0 // tn + j)),
        ],
        out_specs=pl.BlockSpec((tm, tn), lambda j, i: (i, j)),
        out_shape=jax.ShapeDtypeStruct((m, n), out_dtype),
        compiler_params=_params("parallel", "parallel"),
        name="proj_" + (act or "plain"),
    )(a, w_stack)


def _gates_body(a_ref, w_ref, b_ref, o_ref):
    pre = jnp.dot(a_ref[...], w_ref[...], preferred_element_type=F32) + b_ref[...]
    o_ref[...] = GATE_SOFTCAP * jnp.tanh(pre / GATE_SOFTCAP)


def gates_proj(a, w_pad, bias_pad, tm=512):
    m, k = a.shape
    return pl.pallas_call(
        _gates_body,
        grid=(m // tm,),
        in_specs=[
            pl.BlockSpec((tm, k), lambda i: (i, 0)),
            pl.BlockSpec((k, LANES), lambda i: (0, 0)),
            pl.BlockSpec((1, LANES), lambda i: (0, 0)),
        ],
        out_specs=pl.BlockSpec((tm, LANES), lambda i: (i, 0)),
        out_shape=jax.ShapeDtypeStruct((m, LANES), F32),
        compiler_params=_params("parallel"),
        name="gates_proj",
    )(a, w_pad, bias_pad)


def _merge_branches_body(at_ref, hm_ref, sga_ref, sgm_ref, wa_ref, wm_ref, o_ref):
    ya = jnp.dot(at_ref[...], wa_ref[...].astype(BF16), preferred_element_type=F32)
    ym = jnp.dot(hm_ref[...], wm_ref[...].astype(BF16), preferred_element_type=F32)
    o_ref[...] = (sga_ref[...].astype(F32) * ya + sgm_ref[...].astype(F32) * ym).astype(o_ref.dtype)


def merge_branches(attn, hm, sig_gates, wa_stack, wm_stack, layer, tm=512, tn=1024):
    m, ka = attn.shape
    km = hm.shape[1]
    n = wa_stack.shape[2]
    tm, tn = min(tm, m), min(tn, n)
    nb = n // tn
    return pl.pallas_call(
        _merge_branches_body,
        grid=(nb, m // tm),
        in_specs=[
            pl.BlockSpec((tm, ka), lambda j, i: (i, 0)),
            pl.BlockSpec((tm, km), lambda j, i: (i, 0)),
            pl.BlockSpec((tm, tn), lambda j, i: (i, j)),
            pl.BlockSpec((tm, tn), lambda j, i: (i, j + nb)),
            pl.BlockSpec((None, ka, tn), lambda j, i: (layer, 0, j)),
            pl.BlockSpec((None, km, tn), lambda j, i: (layer, 0, j)),
        ],
        out_specs=pl.BlockSpec((tm, tn), lambda j, i: (i, j)),
        out_shape=jax.ShapeDtypeStruct((m, n), BF16),
        compiler_params=_params("parallel", "parallel"),
        name="merge_branches",
    )(attn, hm, sig_gates, sig_gates, wa_stack, wm_stack)


def _out_proj_body(a_ref, w_ref, r_ref, o_ref):
    o_ref[...] = r_ref[...] + jnp.dot(a_ref[...], w_ref[...].astype(BF16), preferred_element_type=F32)


def out_proj_residual(a, w_stack, layer, resid, tm=512, tn=1024):
    m, k = a.shape
    n = w_stack.shape[2]
    tm, tn = min(tm, m), min(tn, n)
    return pl.pallas_call(
        _out_proj_body,
        grid=(n // tn, m // tm),
        in_specs=[
            pl.BlockSpec((tm, k), lambda j, i: (i, 0)),
            pl.BlockSpec((None, k, tn), lambda j, i: (layer, 0, j)),
            pl.BlockSpec((tm, tn), lambda j, i: (i, j)),
        ],
        out_specs=pl.BlockSpec((tm, tn), lambda j, i: (i, j)),
        out_shape=jax.ShapeDtypeStruct((m, n), F32),
        compiler_params=_params("parallel", "parallel"),
        name="out_proj_residual",
    )(a, w_stack, resid)


def _qkv_rope_body(a_ref, w_ref, cos_ref, sin_ref, o_ref):
    acc = jnp.dot(a_ref[...], w_ref[...].astype(BF16), preferred_element_type=F32)
    tm, tn = acc.shape
    c = cos_ref[0]
    s = sin_ref[0]
    first_half = lax.broadcasted_iota(jnp.int32, (tm, LANES), 1) < ROPE_HALF
    for h in range(tn // HEAD_DIM_A):
        hs = slice(h * HEAD_DIM_A, (h + 1) * HEAD_DIM_A)
        t = acc[:, hs]
        partner = jnp.where(first_half, pltpu.roll(t, LANES - ROPE_HALF, 1), pltpu.roll(t, ROPE_HALF, 1))
        o_ref[:, hs] = t * c + partner * s


QKV_TILES_PER_ROLE = 2


def qkv_rope_proj(a, w_stack, layer, cos_tabs, sin_tabs, seq, tm=1024):
    m, k = a.shape
    n = 3 * ATTN_WIDTH
    tpr = QKV_TILES_PER_ROLE
    tn = ATTN_WIDTH // tpr
    tm = min(tm, seq)
    per_seq = seq // tm
    return pl.pallas_call(
        _qkv_rope_body,
        grid=(3 * tpr, m // tm),
        in_specs=[
            pl.BlockSpec((tm, k), lambda j, i: (i, 0)),
            pl.BlockSpec((None, k, tn), lambda j, i: (layer, 0, j)),
            pl.BlockSpec((1, tm, LANES), lambda j, i: (j // tpr, i % per_seq, 0)),
            pl.BlockSpec((1, tm, LANES), lambda j, i: (j // tpr, i % per_seq, 0)),
        ],
        out_specs=pl.BlockSpec((tm, tn), lambda j, i: (i, j)),
        out_shape=jax.ShapeDtypeStruct((m, n), F32),
        compiler_params=_params("parallel", "parallel"),
        name="qkv_rope_proj",
    )(a, w_stack, cos_tabs, sin_tabs)


ATTN_TQ = 128
ATTN_LOAD_ROWS = 256
ATTN_UNROLL = 4


def _attn_body(q_ref, k_ref, v_ref, o_ref, lse_ref, qs, ks, vs, *, seq, dil, radius):
    n = seq // dil
    tq = ATTN_TQ
    kw = tq + 2 * radius
    rb = min(ATTN_LOAD_ROWS, n)
    row = lax.broadcasted_iota(jnp.int32, (tq, kw), 0)
    col = lax.broadcasted_iota(jnp.int32, (tq, kw), 1)

    def residue(r, carry):
        def regroup(i, c):
            a0 = pl.multiple_of(i * rb, rb)
            src = pl.ds(r + a0 * dil, rb, stride=dil)
            qs[pl.ds(a0, rb), :] = q_ref[0, src, :].astype(BF16)
            ks[pl.ds(a0, rb), :] = k_ref[0, src, :].astype(BF16)
            vs[pl.ds(a0, rb), :] = v_ref[0, src, :].astype(BF16)
            return c

        lax.fori_loop(0, n // rb, regroup, 0)

        def q_block(i, c):
            q0 = pl.multiple_of(i * tq, tq)
            ws = pl.multiple_of(jnp.clip(q0 - radius, 0, n - kw), radius)
            qb = qs[pl.ds(q0, tq), :]
            kb = ks[pl.ds(ws, kw), :]
            vb = vs[pl.ds(ws, kw), :]
            s = lax.dot_general(qb, kb, NT_DIMS, preferred_element_type=F32)
            dist = (col + ws) - (row + q0)
            s = jnp.where(jnp.abs(dist) <= radius, s, -jnp.inf)
            mx = jnp.max(s, axis=-1, keepdims=True)
            p = jnp.exp(s - mx)
            l = jnp.sum(p, axis=-1, keepdims=True)
            o = jnp.dot(p.astype(BF16), vb, preferred_element_type=F32) / l
            dst = pl.ds(r + q0 * dil, tq, stride=dil)
            o_ref[0, dst, :] = o
            lse_ref[0, dst, :] = jnp.broadcast_to(mx + jnp.log(l), (tq, LANES))
            return c

        lax.fori_loop(0, n // tq, q_block, 0, unroll=ATTN_UNROLL)
        return carry

    lax.fori_loop(0, dil, residue, 0)


def dilated_attention_group(qkv, group, seq):
    window, dil = ATTN_GROUPS[group]
    radius = window // (2 * dil)
    b = qkv.shape[0]
    n = seq // dil
    hg = HEADS_PER_GROUP
    head0 = group * hg

    def qmap(off):
        return lambda bi, h: (bi, 0, off + head0 + h)

    blk = (1, seq, LANES)
    out_sd = jax.ShapeDtypeStruct((b, seq, hg * LANES), F32)
    o, lse = pl.pallas_call(
        functools.partial(_attn_body, seq=seq, dil=dil, radius=radius),
        grid=(b, hg),
        in_specs=[
            pl.BlockSpec(blk, qmap(0)),
            pl.BlockSpec(blk, qmap(N_ATTN_HEADS)),
            pl.BlockSpec(blk, qmap(2 * N_ATTN_HEADS)),
        ],
        out_specs=[pl.BlockSpec(blk, lambda bi, h: (bi, 0, h)), pl.BlockSpec(blk, lambda bi, h: (bi, 0, h))],
        out_shape=[out_sd, out_sd],
        scratch_shapes=[pltpu.VMEM((n, LANES), BF16)] * 3,
        compiler_params=_params("parallel", "parallel"),
        name=f"dilated_attn_g{group}",
    )(qkv, qkv, qkv)
    width = hg * LANES
    return o.reshape(b * seq, width), lse.reshape(b * seq, width)


def _attn_merge_body(o0, o1, o2, l0, l1, l2, out_ref):
    la, lb, lc = l0[...], l1[...], l2[...]
    mx = jnp.maximum(jnp.maximum(la, lb), lc)
    wa, wb, wc = jnp.exp(la - mx), jnp.exp(lb - mx), jnp.exp(lc - mx)
    acc = wa * o0[...] + wb * o1[...] + wc * o2[...]
    out_ref[...] = (acc / (wa + wb + wc)).astype(out_ref.dtype)


def attn_merge(outs, lses, tm=1024):
    m, w = outs[0].shape
    spec = pl.BlockSpec((tm, w), lambda i: (i, 0))
    return pl.pallas_call(
        _attn_merge_body,
        grid=(m // tm,),
        in_specs=[spec] * 6,
        out_specs=spec,
        out_shape=jax.ShapeDtypeStruct((m, w), BF16),
        compiler_params=_params("parallel"),
        name="attn_merge",
    )(*outs, *lses)


ML_SCAN_KINDS = 6
ML_BLOCK = LANES


def _gate_scan_body(g_ref, o_ref, tot, mst, *, rows, heads):
    rph = rows // heads
    lane = lax.broadcasted_iota(jnp.int32, (rows, LANES), 1)
    shifts = [1 << s for s in range(LANES.bit_length() - 1)]
    qk_scale = ML_DIM ** -0.5

    def log_sigmoid(f):
        return jnp.minimum(f, 0.0) - jnp.log1p(jnp.exp(-jnp.abs(f)))

    def scan(y, combine, identity, reverse):
        for sh in shifts:
            if reverse:
                moved = jnp.where(lane < LANES - sh, pltpu.roll(y, LANES - sh, 1), identity)
            else:
                moved = jnp.where(lane >= sh, pltpu.roll(y, sh, 1), identity)
            y = combine(y, moved)
        return y

    for d, reverse in ((0, False), (1, True)):
        i_pre = g_ref[0, pl.ds((2 * d) * rows, rows), :]
        f_pre = g_ref[0, pl.ds((2 * d + 1) * rows, rows), :]
        b = scan(log_sigmoid(f_pre), jnp.add, 0.0, reverse)
        w = i_pre - b
        pm = scan(w, jnp.maximum, -jnp.inf, reverse)
        last = 0 if reverse else LANES - 1
        tot[0] = jnp.broadcast_to(b[:, last:last + 1], (rows, LANES))
        tot[1] = jnp.broadcast_to(pm[:, last:last + 1], (rows, LANES))

        def step(t, m):
            idx = pl.ds((rph - 1 - t) if reverse else t, heads, stride=rph)
            mst[idx, :] = m
            return tot[0, idx, :] + jnp.maximum(m, tot[1, idx, :])

        lax.fori_loop(0, rph, step, jnp.full((heads, LANES), -jnp.inf, F32))
        m_in = mst[...]
        mt = jnp.maximum(m_in, pm)
        mx = jnp.maximum(m_in, tot[1])
        base = ML_SCAN_KINDS * d
        o_ref[0, base + 0] = mt
        o_ref[0, base + 1] = jnp.exp(m_in - mt)
        o_ref[0, base + 2] = jnp.exp(-(b + mt))
        o_ref[0, base + 3] = jnp.exp(m_in - mx)
        o_ref[0, base + 4] = jnp.exp(w - mx) * qk_scale
        o_ref[0, base + 5] = w


def gate_scan(gates_t, rows, heads):
    b = gates_t.shape[0]
    kinds = 2 * ML_SCAN_KINDS
    return pl.pallas_call(
        functools.partial(_gate_scan_body, rows=rows, heads=heads),
        grid=(b,),
        in_specs=[pl.BlockSpec((1, 4 * rows, LANES), lambda i: (i, 0, 0))],
        out_specs=pl.BlockSpec((1, kinds, rows, LANES), lambda i: (i, 0, 0, 0)),
        out_shape=jax.ShapeDtypeStruct((b, kinds, rows, LANES), F32),
        scratch_shapes=[pltpu.VMEM((2, rows, LANES), F32), pltpu.VMEM((rows, LANES), F32)],
        compiler_params=_params("parallel"),
        name="gate_scan",
    )(gates_t)


ML_PARAMS = 16
ML_N_ROWS = 8
ML_NORM_ROWS = 256


def _mlstm_body(q_ref, k_ref, v_ref, mo_ref, par_ref, gain_ref, o_ref, hf, hb, cf, cb, nf, nb, *, seq):
    L = ML_BLOCK
    nblk = seq // L
    cf[...] = jnp.zeros_like(cf)
    cb[...] = jnp.zeros_like(cb)
    nf[...] = jnp.zeros_like(nf)
    nb[...] = jnp.zeros_like(nb)
    ti = lax.broadcasted_iota(jnp.int32, (L, L), 0)
    si = lax.broadcasted_iota(jnp.int32, (L, L), 1)

    def block(r0, base, c_ref, n_ref, h_ref, causal):
        qc = q_ref[0, pl.ds(r0, L), :]
        kc = k_ref[0, pl.ds(r0, L), :]
        vc = v_ref[0, pl.ds(r0, L), :]
        par = par_ref[0, 0, :, pl.ds(r0, L)]
        cols = par.T
        mt, w_inter, e_negm = (cols[:, base + i:base + i + 1] for i in range(3))
        decay = par[base + 3:base + 4, 0:1]
        ws_row = par[base + 4:base + 5, :]
        w_row = par[base + 5:base + 6, :]
        dmat = jnp.where(causal, jnp.exp(w_row - mt), 0.0)
        s = lax.dot_general(qc, kc, NT_DIMS, preferred_element_type=F32) * (ML_DIM ** -0.5)
        a = s * dmat
        intra = jnp.dot(a.astype(BF16), vc, preferred_element_type=F32)
        row_sum = jnp.sum(a, axis=1, keepdims=True)
        inter = jnp.dot(qc, c_ref[...].astype(BF16), preferred_element_type=F32)
        qn = lax.dot_general(qc, n_ref[...].astype(BF16), NT_DIMS, preferred_element_type=F32)[:, 0:1]
        num = w_inter * inter + intra
        den = w_inter * qn + row_sum
        h_ref[pl.ds(r0, L), :] = num / jnp.maximum(jnp.abs(den), e_negm)
        ws_rows = jnp.broadcast_to(ws_row, (ML_N_ROWS, L)).astype(BF16)
        n_ref[...] = decay * n_ref[...] + jnp.dot(ws_rows, kc, preferred_element_type=F32)
        ks_t = (kc.T.astype(F32) * ws_row).astype(BF16)
        for half in range(ML_DIM // LANES):
            rs = slice(half * LANES, (half + 1) * LANES)
            kv = jnp.dot(ks_t[rs, :], vc, preferred_element_type=F32)
            c_ref[rs, :] = decay * c_ref[rs, :] + kv

    def step(j, carry):
        block(pl.multiple_of(j * L, L), 0, cf, nf, hf, ti >= si)
        block(pl.multiple_of((nblk - 1 - j) * L, L), ML_SCAN_KINDS, cb, nb, hb, ti <= si)
        return carry

    lax.fori_loop(0, nblk, step, 0)

    rb = min(ML_NORM_ROWS, seq)

    def finish(i, carry):
        r0 = pl.multiple_of(i * rb, rb)
        h = hf[pl.ds(r0, rb), :] + hb[pl.ds(r0, rb), :]
        h = h * lax.rsqrt(jnp.mean(h * h, axis=-1, keepdims=True) + EPS)
        og = jax.nn.sigmoid(mo_ref[0, pl.ds(r0, rb), :].astype(F32))
        o_ref[0, pl.ds(r0, rb), :] = (h * gain_ref[...] * og).astype(o_ref.dtype)
        return carry

    lax.fori_loop(0, seq // rb, finish, 0)


def mlstm(proj, params, gain, seq):
    b = proj.shape[0]

    def pmap(off):
        return lambda bi, h: (bi, 0, off * ML_HEADS + h)

    blk = (1, seq, ML_DIM)
    return pl.pallas_call(
        functools.partial(_mlstm_body, seq=seq),
        grid=(b, ML_HEADS),
        in_specs=[
            pl.BlockSpec(blk, pmap(0)),
            pl.BlockSpec(blk, pmap(1)),
            pl.BlockSpec(blk, pmap(2)),
            pl.BlockSpec(blk, pmap(3)),
            pl.BlockSpec((1, 1, ML_PARAMS, seq), lambda bi, h: (bi, h, 0, 0)),
            pl.BlockSpec((1, ML_DIM), lambda bi, h: (0, h)),
        ],
        out_specs=pl.BlockSpec(blk, lambda bi, h: (bi, 0, h)),
        out_shape=jax.ShapeDtypeStruct((b, seq, ML_WIDTH), BF16),
        scratch_shapes=[
            pltpu.VMEM((seq, ML_DIM), F32),
            pltpu.VMEM((seq, ML_DIM), F32),
            pltpu.VMEM((ML_DIM, ML_DIM), F32),
            pltpu.VMEM((ML_DIM, ML_DIM), F32),
            pltpu.VMEM((ML_N_ROWS, ML_DIM), F32),
            pltpu.VMEM((ML_N_ROWS, ML_DIM), F32),
        ],
        compiler_params=_params("parallel", "parallel"),
        name="mlstm",
    )(proj, proj, proj, proj, params, gain.reshape(1, ML_WIDTH))


TOPK_PARTS = 16
TOPK_UNROLL = 4


def _topk_body(aff_ref, idx_ref, gate_ref, pos_s, *, n_experts, rows, cap):
    a3 = aff_ref[0]
    bits3 = pltpu.bitcast(a3, jnp.int32)

    def count(mask3):
        c = jnp.sum(mask3.astype(jnp.int32), axis=2, keepdims=True)
        return jnp.sum(c, axis=1, keepdims=True)

    def search(i, t):
        cand = t | jnp.left_shift(jnp.int32(1), 30 - i)
        return jnp.where(count(bits3 >= cand) >= cap, cand, t)

    thr = lax.fori_loop(0, 31, search, jnp.zeros((n_experts, 1, 1), jnp.int32))
    need = (cap - count(bits3 > thr)).astype(F32)

    er = n_experts * rows
    shape3 = (n_experts, rows, LANES)
    bits2 = bits3.reshape(er, LANES)
    thr2 = jnp.broadcast_to(thr, shape3).reshape(er, LANES)
    need2 = jnp.broadcast_to(need, shape3).reshape(er, LANES)
    ri = lax.broadcasted_iota(jnp.int32, (LANES, LANES), 0)
    ci = lax.broadcasted_iota(jnp.int32, (LANES, LANES), 1)
    upper = jnp.where(ri <= ci, 1.0, 0.0).astype(BF16)
    gi = lax.broadcasted_iota(jnp.int32, (er, er), 0)
    gj = lax.broadcasted_iota(jnp.int32, (er, er), 1)
    same_expert = (gi - gj) < (rows - (gj & (rows - 1)))
    earlier_rows = jnp.where((gj < gi) & same_expert, 1.0, 0.0).astype(BF16)

    def exclusive_prefix(mask_f):
        incl = jnp.dot(mask_f.astype(BF16), upper, preferred_element_type=F32)
        row_tot = jnp.broadcast_to(incl[:, LANES - 1:LANES], (er, LANES)).astype(BF16)
        before = jnp.dot(earlier_rows, row_tot, preferred_element_type=F32)
        return incl + before - mask_f

    eq_f = jnp.where(bits2 == thr2, 1.0, 0.0)
    tie_ok = (bits2 == thr2) & (exclusive_prefix(eq_f) < need2)
    sel = (bits2 > thr2) | tie_ok
    sel_f = jnp.where(sel, 1.0, 0.0)
    pos_s[...] = jnp.where(sel, exclusive_prefix(sel_f), -1.0).reshape(shape3)

    slot = lax.broadcasted_iota(jnp.int32, (cap, LANES), 0).astype(F32)
    lane_i = lax.broadcasted_iota(jnp.int32, (1, LANES), 1)
    part = lax.broadcasted_iota(jnp.int32, (TOPK_PARTS, LANES), 0)

    def per_expert(e, carry):
        def per_row(j, acc):
            hit = jnp.where(pos_s[e, pl.ds(j, 1), :] == slot, 1.0, 0.0).astype(BF16)
            a_row = aff_ref[0, e, pl.ds(j, 1), :]
            tok = lane_i + j * LANES
            a1 = a_row.astype(BF16).astype(F32)
            r1 = a_row - a1
            a2 = r1.astype(BF16).astype(F32)
            parts = [(tok >> 6).astype(F32), (tok & 63).astype(F32), a1, a2, r1 - a2]
            lhs = jnp.zeros((TOPK_PARTS, LANES), F32)
            for p, v in enumerate(parts):
                lhs = jnp.where(part == p, v, lhs)
            return acc + lax.dot_general(lhs.astype(BF16), hit, NT_DIMS, preferred_element_type=F32)

        acc = lax.fori_loop(0, rows, per_row, jnp.zeros((TOPK_PARTS, cap), F32), unroll=TOPK_UNROLL)
        idx_ref[0, e] = (acc[0:1] * 64.0 + acc[1:2]).astype(jnp.int32)
        gate_ref[0, e] = (acc[2:3] + acc[3:4]) + acc[4:5]
        return carry

    lax.fori_loop(0, n_experts, per_expert, 0)


def expert_topk(aff_t, cap):
    b, e, rows, _ = aff_t.shape
    assert rows & (rows - 1) == 0, "tokens per expert row group must be a power of two"
    return pl.pallas_call(
        functools.partial(_topk_body, n_experts=e, rows=rows, cap=cap),
        grid=(b,),
        in_specs=[pl.BlockSpec((1, e, rows, LANES), lambda i: (i, 0, 0, 0))],
        out_specs=[
            pl.BlockSpec((1, e, 1, cap), lambda i: (i, 0, 0, 0)),
            pl.BlockSpec((1, e, 1, cap), lambda i: (i, 0, 0, 0)),
        ],
        out_shape=[jax.ShapeDtypeStruct((b, e, 1, cap), jnp.int32), jax.ShapeDtypeStruct((b, e, 1, cap), F32)],
        scratch_shapes=[pltpu.VMEM((e, rows, LANES), F32)],
        compiler_params=_params("parallel"),
        name="expert_topk",
    )(aff_t)


DMA_UNROLL = 8
DOWN_COLS = 512


def _expert_ffn_body(idx_ref, prev_idx_ref, gate_ref, xn_hbm, x_hbm, wg_ref, wu_ref, wd_ref, o_hbm,
                     xe, xr, sems, *, seq, cap, n_steps):
    del x_hbm
    bsz = pl.num_programs(1)
    step = pl.program_id(0) * bsz + pl.program_id(1)
    slot = step % 2
    base = pl.program_id(1) * seq
    prev_base = ((pl.program_id(1) + bsz - 1) % bsz) * seq

    def xn_copy(c):
        row = base + idx_ref[0, 0, 0, c]
        return pltpu.make_async_copy(xn_hbm.at[pl.ds(row, 1), :], xe.at[pl.ds(c, 1), :], sems.at[0])

    def x_copy(c):
        row = base + idx_ref[0, 0, 0, c]
        return pltpu.make_async_copy(o_hbm.at[pl.ds(row, 1), :], xr.at[slot, pl.ds(c, 1), :], sems.at[1])

    def out_copy(c):
        row = base + idx_ref[0, 0, 0, c]
        return pltpu.make_async_copy(xr.at[slot, pl.ds(c, 1), :], o_hbm.at[pl.ds(row, 1), :], sems.at[2])

    def prev_out_copy(c):
        row = prev_base + prev_idx_ref[0, 0, 0, c]
        return pltpu.make_async_copy(xr.at[1 - slot, pl.ds(c, 1), :], o_hbm.at[pl.ds(row, 1), :], sems.at[2])

    def for_all(fn):
        lax.fori_loop(0, cap, lambda c, carry: (fn(c), carry)[1], 0, unroll=DMA_UNROLL)

    for_all(lambda c: (xn_copy(c).start(), x_copy(c).start()))

    @pl.when(step > 0)
    def _():
        for_all(lambda c: prev_out_copy(c).wait())

    for_all(lambda c: xn_copy(c).wait())
    x_in = xe[...].astype(BF16)
    hg = jnp.dot(x_in, wg_ref[0], preferred_element_type=F32)
    hu = jnp.dot(x_in, wu_ref[0], preferred_element_type=F32)
    hid = (jax.nn.silu(hg) * hu * gate_ref[0, 0]).astype(BF16)
    for_all(lambda c: x_copy(c).wait())
    d = xe.shape[1]
    dc = min(DOWN_COLS, d)
    for j in range(d // dc):
        cs = slice(j * dc, (j + 1) * dc)
        xr[slot, :, cs] += jnp.dot(hid, wd_ref[0, :, cs], preferred_element_type=F32)

    for_all(lambda c: out_copy(c).start())

    @pl.when(step == n_steps - 1)
    def _():
        for_all(lambda c: out_copy(c).wait())


def expert_ffn(idx, gate, xn, x, wg, wu, wd, seq):
    bsz, n_e, cap, _ = gate.shape
    m, d = x.shape
    f = wg.shape[2]
    assert bsz >= 2, "consecutive grid steps must work on different batches"

    def prev_step(e, b):
        first = (e == 0) & (b == 0)
        pb = jnp.where(first, 0, (b + bsz - 1) % bsz)
        pe = jnp.where(first | (b > 0), e, e - 1)
        return pb, pe, 0, 0

    return pl.pallas_call(
        functools.partial(_expert_ffn_body, seq=seq, cap=cap, n_steps=n_e * bsz),
        grid=(n_e, bsz),
        in_specs=[
            pl.BlockSpec((1, 1, 1, cap), lambda e, b: (b, e, 0, 0), memory_space=pltpu.SMEM),
            pl.BlockSpec((1, 1, 1, cap), prev_step, memory_space=pltpu.SMEM),
            pl.BlockSpec((1, 1, cap, 1), lambda e, b: (b, e, 0, 0)),
            pl.BlockSpec(memory_space=pl.ANY),
            pl.BlockSpec(memory_space=pl.ANY),
            pl.BlockSpec((1, d, f), lambda e, b: (e, 0, 0)),
            pl.BlockSpec((1, d, f), lambda e, b: (e, 0, 0)),
            pl.BlockSpec((1, f, d), lambda e, b: (e, 0, 0)),
        ],
        out_specs=pl.BlockSpec(memory_space=pl.ANY),
        out_shape=jax.ShapeDtypeStruct((m, d), F32),
        scratch_shapes=[
            pltpu.VMEM((cap, d), F32),
            pltpu.VMEM((2, cap, d), F32),
            pltpu.SemaphoreType.DMA((3,)),
        ],
        input_output_aliases={4: 0},
        compiler_params=_params("arbitrary", "arbitrary"),
        name="expert_ffn",
    )(idx, idx, gate, xn, x, wg, wu, wd)


def _rope_tables(seq):
    inv_freq = ROPE_THETA ** (-jnp.arange(0, ROPE_DIM, 2, dtype=F32) / ROPE_DIM)
    ang = jnp.arange(seq, dtype=F32)[:, None] * inv_freq[None, :]
    cos, sin = jnp.cos(ang), jnp.sin(ang)
    rest = HEAD_DIM_A - ROPE_DIM
    cos_t = jnp.concatenate([cos, cos, jnp.ones((seq, rest), F32)], axis=1)
    sin_t = jnp.concatenate([-sin, sin, jnp.zeros((seq, rest), F32)], axis=1)
    scale = HEAD_DIM_A ** -0.5
    cos_tabs = jnp.stack([cos_t * scale, cos_t, jnp.ones_like(cos_t)])
    sin_tabs = jnp.stack([sin_t * scale, sin_t, jnp.zeros_like(sin_t)])
    return cos_tabs, sin_tabs


def _pad_cols(w, width):
    return jnp.pad(w, ((0, 0), (0, width - w.shape[1])))


def _mixer(x2d, xn, bsz, seq, layer, w_in, w_branch_gates, gate_bias, ml_gain, w_attn, w_mlstm, w_out, rope_tabs):
    m = bsz * seq
    d_model = x2d.shape[1]
    qkv_w = 3 * ATTN_WIDTH
    w_gates = _pad_cols(w_in[layer, :, MAIN_WIDTH:MAIN_WIDTH + N_GATES], LANES).astype(BF16)
    bias_pad = _pad_cols(gate_bias.reshape(1, N_GATES), LANES)

    qkv = qkv_rope_proj(xn, w_in, layer, *rope_tabs, seq)
    ml_proj = matmul(xn, w_in, layer, qkv_w, 4 * ML_WIDTH, BF16, tm=2048, tn=512)
    gates = gates_proj(xn, w_gates, bias_pad)
    sig = matmul(xn, w_branch_gates, layer, 0, 2 * d_model, BF16, tm=512, tn=1024, act="sigmoid")

    qkv3 = qkv.reshape(bsz, seq, qkv_w)
    outs, lses = [], []
    for g in range(len(ATTN_GROUPS)):
        o, lse = dilated_attention_group(qkv3, g, seq)
        outs.append(o)
        lses.append(lse)
    attn = attn_merge(outs, lses)

    rows = seq // LANES
    gates_t = gates[:, :N_GATES].reshape(bsz, seq, N_GATES).transpose(0, 2, 1).reshape(bsz, N_GATES * rows, LANES)
    scans = gate_scan(gates_t, ML_HEADS * rows, ML_HEADS)
    params = scans.reshape(bsz, 2 * ML_SCAN_KINDS, ML_HEADS, seq).transpose(0, 2, 1, 3)
    params = jnp.pad(params, ((0, 0), (0, 0), (0, ML_PARAMS - 2 * ML_SCAN_KINDS), (0, 0)))
    hm = mlstm(ml_proj.reshape(bsz, seq, 4 * ML_WIDTH), params, ml_gain, seq).reshape(m, ML_WIDTH)

    merged = merge_branches(attn, hm, sig, w_attn, w_mlstm, layer)
    return out_proj_residual(merged, w_out, layer, x2d)


def _ffn(x2d, bsz, seq, gain, w_router, w_gate, w_up, w_down):
    n_e = w_router.shape[1]
    cap = CAPACITY_FACTOR * seq // n_e
    xn, aff = rmsnorm_router(x2d, gain, _pad_cols(w_router, LANES).astype(BF16), n_e)
    aff_t = aff[:, :n_e].reshape(bsz, seq, n_e).transpose(0, 2, 1).reshape(bsz, n_e, seq // LANES, LANES)
    idx, gate = expert_topk(aff_t, cap)
    return expert_ffn(idx, gate.reshape(bsz, n_e, cap, 1), xn, x2d,
                      w_gate.astype(BF16), w_up.astype(BF16), w_down.astype(BF16), seq)


def kernel(x, w_in, ml_gate_bias, ml_norm_gain, w_attn_branch, w_mlstm_branch, w_out, norm_mix_gain,
           norm_ffn_gain, w_router, w_expert_gate, w_expert_up, w_expert_down, final_norm_gain):
    bsz, seq, d_model = x.shape
    depth = w_in.shape[0]
    rope_tabs = _rope_tables(seq)
    w_branch_gates = w_in[:, :, MAIN_WIDTH + N_GATES:]
    x2d = x.reshape(bsz * seq, d_model)
    for layer in range(depth):
        xn = rmsnorm(x2d, norm_mix_gain[layer], BF16)
        x2d = _mixer(x2d, xn, bsz, seq, layer, w_in, w_branch_gates, ml_gate_bias[layer], ml_norm_gain[layer],
                     w_attn_branch, w_mlstm_branch, w_out, rope_tabs)
        x2d = _ffn(x2d, bsz, seq, norm_ffn_gain[layer], w_router[layer], w_expert_gate[layer],
                   w_expert_up[layer], w_expert_down[layer])
    return rmsnorm(x2d, final_norm_gain, F32).reshape(bsz, seq, d_model)
```

```python
import functools

import jax
import jax.numpy as jnp
from jax import lax
from jax.experimental import pallas as pl
from jax.experimental.pallas import tpu as pltpu

F32 = jnp.float32
BF16 = jnp.bfloat16

ATTN_GROUPS = ((128, 1), (512, 4), (2048, 16))
HEADS_PER_GROUP = 4
HEAD_DIM_A = 128
N_ATTN_HEADS = len(ATTN_GROUPS) * HEADS_PER_GROUP
ATTN_WIDTH = N_ATTN_HEADS * HEAD_DIM_A
ATTN_OUT_WIDTH = HEADS_PER_GROUP * HEAD_DIM_A
ROPE_DIM = HEAD_DIM_A // 4
ROPE_HALF = ROPE_DIM // 2
ROPE_THETA = 500000.0
ML_HEADS = 8
ML_DIM = 256
ML_WIDTH = ML_HEADS * ML_DIM
ML_CHUNK = 64
GATE_SOFTCAP = 15.0
N_EXPERTS = 16
CAPACITY_FACTOR = 2
EPS = 1e-6
MAIN_WIDTH = 3 * ATTN_WIDTH + 4 * ML_WIDTH
N_GATES = 4 * ML_HEADS

LANES = 128
SUBLANES = 8
VMEM_LIMIT_BYTES = 60 * 1024 * 1024

NT_DIMS = (((1,), (1,)), ((), ()))
TN_DIMS = (((0,), (0,)), ((), ()))


def _params(*sem):
    return pltpu.CompilerParams(dimension_semantics=sem, vmem_limit_bytes=VMEM_LIMIT_BYTES)


def _rmsnorm_body(x_ref, g_ref, o_ref):
    x = x_ref[...]
    inv = lax.rsqrt(jnp.mean(x * x, axis=-1, keepdims=True) + EPS)
    o_ref[...] = (x * inv * g_ref[...]).astype(o_ref.dtype)


def rmsnorm(x2d, gain, out_dtype, tm=256):
    m, d = x2d.shape
    return pl.pallas_call(
        _rmsnorm_body,
        grid=(m // tm,),
        in_specs=[pl.BlockSpec((tm, d), lambda i: (i, 0)), pl.BlockSpec((1, d), lambda i: (0, 0))],
        out_specs=pl.BlockSpec((tm, d), lambda i: (i, 0)),
        out_shape=jax.ShapeDtypeStruct((m, d), out_dtype),
        compiler_params=_params("parallel"),
        name="rmsnorm",
    )(x2d, gain.reshape(1, d))


def _rmsnorm_router_body(x_ref, g_ref, wr_ref, xn_ref, aff_ref, *, n_experts):
    x = x_ref[...]
    inv = lax.rsqrt(jnp.mean(x * x, axis=-1, keepdims=True) + EPS)
    xn = x * inv * g_ref[...]
    xn_ref[...] = xn
    logits = jnp.dot(xn.astype(BF16), wr_ref[...], preferred_element_type=F32)
    lane = lax.broadcasted_iota(jnp.int32, logits.shape, 1)
    logits = jnp.where(lane < n_experts, logits, -jnp.inf)
    mx = jnp.max(logits, axis=-1, keepdims=True)
    p = jnp.exp(logits - mx)
    aff_ref[...] = p / jnp.sum(p, axis=-1, keepdims=True)


def rmsnorm_router(x2d, gain, w_router_pad, n_experts, tm=256):
    m, d = x2d.shape
    return pl.pallas_call(
        functools.partial(_rmsnorm_router_body, n_experts=n_experts),
        grid=(m // tm,),
        in_specs=[
            pl.BlockSpec((tm, d), lambda i: (i, 0)),
            pl.BlockSpec((1, d), lambda i: (0, 0)),
            pl.BlockSpec((d, LANES), lambda i: (0, 0)),
        ],
        out_specs=[pl.BlockSpec((tm, d), lambda i: (i, 0)), pl.BlockSpec((tm, LANES), lambda i: (i, 0))],
        out_shape=[jax.ShapeDtypeStruct((m, d), F32), jax.ShapeDtypeStruct((m, LANES), F32)],
        compiler_params=_params("parallel"),
        name="rmsnorm_router",
    )(x2d, gain.reshape(1, d), w_router_pad)


def _wt_spec(layer, row0, tn, k):
    assert row0 % SUBLANES == 0 and tn % SUBLANES == 0
    return pl.BlockSpec((pl.Element(1), pl.Element(tn), pl.Element(k)),
                        lambda j, i: (layer, pl.multiple_of(row0 + j * tn, SUBLANES), 0))


def _matmul_body(a_ref, w_ref, o_ref, *, act):
    acc = lax.dot_general(a_ref[...], w_ref[0].astype(BF16), NT_DIMS, preferred_element_type=F32)
    if act == "sigmoid":
        acc = jax.nn.sigmoid(acc)
    o_ref[...] = acc.astype(o_ref.dtype)


def matmul(a, wt_stack, layer, row0, n, out_dtype, tm, tn, act=None):
    m, k = a.shape
    tm, tn = min(tm, m), min(tn, n)
    assert n % tn == 0
    return pl.pallas_call(
        functools.partial(_matmul_body, act=act),
        grid=(n // tn, m // tm),
        in_specs=[pl.BlockSpec((tm, k), lambda j, i: (i, 0)), _wt_spec(layer, row0, tn, k)],
        out_specs=pl.BlockSpec((tm, tn), lambda j, i: (i, j)),
        out_shape=jax.ShapeDtypeStruct((m, n), out_dtype),
        compiler_params=_params("parallel", "parallel"),
        name="proj_" + (act or "plain"),
    )(a, wt_stack)


def _gates_body(a_ref, w_ref, b_ref, o_ref):
    pre = lax.dot_general(a_ref[...], w_ref[0].astype(BF16), NT_DIMS, preferred_element_type=F32) + b_ref[...]
    o_ref[...] = GATE_SOFTCAP * jnp.tanh(pre / GATE_SOFTCAP)


def gates_proj(a, wt_stack, layer, row0, bias, tm=512):
    m, k = a.shape
    tm = min(tm, m)
    return pl.pallas_call(
        _gates_body,
        grid=(1, m // tm),
        in_specs=[
            pl.BlockSpec((tm, k), lambda j, i: (i, 0)),
            _wt_spec(layer, row0, N_GATES, k),
            pl.BlockSpec((1, N_GATES), lambda j, i: (0, 0)),
        ],
        out_specs=pl.BlockSpec((tm, N_GATES), lambda j, i: (i, 0)),
        out_shape=jax.ShapeDtypeStruct((m, N_GATES), F32),
        compiler_params=_params("parallel", "parallel"),
        name="gates_proj",
    )(a, wt_stack, bias.reshape(1, N_GATES))


def _merge_branches_body(at_ref, hm_ref, sga_ref, sgm_ref, wa_ref, wm_ref, o_ref):
    ya = jnp.dot(at_ref[...], wa_ref[...].astype(BF16), preferred_element_type=F32)
    ym = jnp.dot(hm_ref[...], wm_ref[...].astype(BF16), preferred_element_type=F32)
    o_ref[...] = (sga_ref[...].astype(F32) * ya + sgm_ref[...].astype(F32) * ym).astype(o_ref.dtype)


def merge_branches(attn, hm, sig_gates, wa_stack, wm_stack, layer, tm=512, tn=1024):
    m, ka = attn.shape
    km = hm.shape[1]
    n = wa_stack.shape[2]
    tm, tn = min(tm, m), min(tn, n)
    nb = n // tn
    return pl.pallas_call(
        _merge_branches_body,
        grid=(nb, m // tm),
        in_specs=[
            pl.BlockSpec((tm, ka), lambda j, i: (i, 0)),
            pl.BlockSpec((tm, km), lambda j, i: (i, 0)),
            pl.BlockSpec((tm, tn), lambda j, i: (i, j)),
            pl.BlockSpec((tm, tn), lambda j, i: (i, j + nb)),
            pl.BlockSpec((None, ka, tn), lambda j, i: (layer, 0, j)),
            pl.BlockSpec((None, km, tn), lambda j, i: (layer, 0, j)),
        ],
        out_specs=pl.BlockSpec((tm, tn), lambda j, i: (i, j)),
        out_shape=jax.ShapeDtypeStruct((m, n), BF16),
        compiler_params=_params("parallel", "parallel"),
        name="merge_branches",
    )(attn, hm, sig_gates, sig_gates, wa_stack, wm_stack)


def _out_proj_body(a_ref, w_ref, r_ref, o_ref):
    o_ref[...] = r_ref[...] + jnp.dot(a_ref[...], w_ref[...].astype(BF16), preferred_element_type=F32)


def out_proj_residual(a, w_stack, layer, resid, tm=512, tn=1024):
    m, k = a.shape
    n = w_stack.shape[2]
    tm, tn = min(tm, m), min(tn, n)
    return pl.pallas_call(
        _out_proj_body,
        grid=(n // tn, m // tm),
        in_specs=[
            pl.BlockSpec((tm, k), lambda j, i: (i, 0)),
            pl.BlockSpec((None, k, tn), lambda j, i: (layer, 0, j)),
            pl.BlockSpec((tm, tn), lambda j, i: (i, j)),
        ],
        out_specs=pl.BlockSpec((tm, tn), lambda j, i: (i, j)),
        out_shape=jax.ShapeDtypeStruct((m, n), F32),
        compiler_params=_params("parallel", "parallel"),
        name="out_proj_residual",
    )(a, w_stack, resid)


def _qkv_rope_body(a_ref, w_ref, cos_ref, sin_ref, o_ref):
    acc = lax.dot_general(a_ref[...], w_ref[0].astype(BF16), NT_DIMS, preferred_element_type=F32)
    tm, tn = acc.shape
    c = cos_ref[0]
    s = sin_ref[0]
    first_half = lax.broadcasted_iota(jnp.int32, (tm, LANES), 1) < ROPE_HALF
    for h in range(tn // HEAD_DIM_A):
        hs = slice(h * HEAD_DIM_A, (h + 1) * HEAD_DIM_A)
        t = acc[:, hs]
        partner = jnp.where(first_half, pltpu.roll(t, LANES - ROPE_HALF, 1), pltpu.roll(t, ROPE_HALF, 1))
        o_ref[:, hs] = t * c + partner * s


QKV_TILES_PER_ROLE = 2


def qkv_rope_proj(a, wt_stack, layer, cos_tabs, sin_tabs, seq, tm=1024):
    m, k = a.shape
    n = 3 * ATTN_WIDTH
    tpr = QKV_TILES_PER_ROLE
    tn = ATTN_WIDTH // tpr
    tm = min(tm, seq)
    per_seq = seq // tm
    return pl.pallas_call(
        _qkv_rope_body,
        grid=(3 * tpr, m // tm),
        in_specs=[
            pl.BlockSpec((tm, k), lambda j, i: (i, 0)),
            _wt_spec(layer, 0, tn, k),
            pl.BlockSpec((1, tm, LANES), lambda j, i: (j // tpr, i % per_seq, 0)),
            pl.BlockSpec((1, tm, LANES), lambda j, i: (j // tpr, i % per_seq, 0)),
        ],
        out_specs=pl.BlockSpec((tm, tn), lambda j, i: (i, j)),
        out_shape=jax.ShapeDtypeStruct((m, n), F32),
        compiler_params=_params("parallel", "parallel"),
        name="qkv_rope_proj",
    )(a, wt_stack, cos_tabs, sin_tabs)


ATTN_TQ = 128
ATTN_LOAD_ROWS = 256
ATTN_UNROLL = 4
ATTN_RESIDUES = 2


def _attn_body(q_ref, k_ref, v_ref, o_ref, lse_ref, qs2, ks2, vs2, *, seq, dil, radius):
    n = seq // dil
    tq = ATTN_TQ
    kw = tq + 2 * radius
    rb = min(ATTN_LOAD_ROWS, n)
    row = lax.broadcasted_iota(jnp.int32, (tq, kw), 0)
    col = lax.broadcasted_iota(jnp.int32, (tq, kw), 1)

    def residue(r, buf):
        qs, ks, vs = qs2.at[buf], ks2.at[buf], vs2.at[buf]

        def regroup(i, c):
            a0 = pl.multiple_of(i * rb, rb)
            src = pl.ds(r + a0 * dil, rb, stride=dil)
            qs[pl.ds(a0, rb), :] = q_ref[0, src, :].astype(BF16)
            ks[pl.ds(a0, rb), :] = k_ref[0, src, :].astype(BF16)
            vs[pl.ds(a0, rb), :] = v_ref[0, src, :].astype(BF16)
            return c

        lax.fori_loop(0, n // rb, regroup, 0)

        def q_block(i, c):
            q0 = pl.multiple_of(i * tq, tq)
            ws = pl.multiple_of(jnp.clip(q0 - radius, 0, n - kw), radius)
            qb = qs[pl.ds(q0, tq), :]
            kb = ks[pl.ds(ws, kw), :]
            vb = vs[pl.ds(ws, kw), :]
            s = lax.dot_general(qb, kb, NT_DIMS, preferred_element_type=F32)
            dist = (col + ws) - (row + q0)
            s = jnp.where(jnp.abs(dist) <= radius, s, -jnp.inf)
            mx = jnp.max(s, axis=-1, keepdims=True)
            p = jnp.exp(s - mx)
            l = jnp.sum(p, axis=-1, keepdims=True)
            o = jnp.dot(p.astype(BF16), vb, preferred_element_type=F32) / l
            dst = pl.ds(r + q0 * dil, tq, stride=dil)
            o_ref[0, dst, :] = o
            lse_ref[0, dst, :] = jnp.broadcast_to(mx + jnp.log(l), (tq, LANES))
            return c

        lax.fori_loop(0, n // tq, q_block, 0, unroll=ATTN_UNROLL)

    ways = min(ATTN_RESIDUES, dil)

    def residues(i, carry):
        for w in range(ways):
            residue(i * ways + w, w)
        return carry

    lax.fori_loop(0, dil // ways, residues, 0)


def dilated_attention_group(qkv, group, seq):
    window, dil = ATTN_GROUPS[group]
    radius = window // (2 * dil)
    b = qkv.shape[0]
    n = seq // dil
    hg = HEADS_PER_GROUP
    head0 = group * hg

    def qmap(off):
        return lambda bi, h: (bi, 0, off + head0 + h)

    blk = (1, seq, LANES)
    out_sd = jax.ShapeDtypeStruct((b, seq, hg * LANES), F32)
    o, lse = pl.pallas_call(
        functools.partial(_attn_body, seq=seq, dil=dil, radius=radius),
        grid=(b, hg),
        in_specs=[
            pl.BlockSpec(blk, qmap(0)),
            pl.BlockSpec(blk, qmap(N_ATTN_HEADS)),
            pl.BlockSpec(blk, qmap(2 * N_ATTN_HEADS)),
        ],
        out_specs=[pl.BlockSpec(blk, lambda bi, h: (bi, 0, h)), pl.BlockSpec(blk, lambda bi, h: (bi, 0, h))],
        out_shape=[out_sd, out_sd],
        scratch_shapes=[pltpu.VMEM((ATTN_RESIDUES, n, LANES), BF16)] * 3,
        compiler_params=_params("parallel", "parallel"),
        name=f"dilated_attn_g{group}",
    )(qkv, qkv, qkv)
    width = hg * LANES
    return o.reshape(b * seq, width), lse.reshape(b * seq, width)


def _attn_merge_body(o0, o1, o2, l0, l1, l2, out_ref):
    la, lb, lc = l0[...], l1[...], l2[...]
    mx = jnp.maximum(jnp.maximum(la, lb), lc)
    wa, wb, wc = jnp.exp(la - mx), jnp.exp(lb - mx), jnp.exp(lc - mx)
    acc = wa * o0[...] + wb * o1[...] + wc * o2[...]
    out_ref[...] = (acc / (wa + wb + wc)).astype(out_ref.dtype)


def attn_merge(outs, lses, tm=1024):
    m, w = outs[0].shape
    spec = pl.BlockSpec((tm, w), lambda i: (i, 0))
    return pl.pallas_call(
        _attn_merge_body,
        grid=(m // tm,),
        in_specs=[spec] * 6,
        out_specs=spec,
        out_shape=jax.ShapeDtypeStruct((m, w), BF16),
        compiler_params=_params("parallel"),
        name="attn_merge",
    )(*outs, *lses)


ML_SCAN_KINDS = 6
ML_BLOCK = LANES


def _gate_scan_body(g_ref, o_ref, tot, mst, *, rows, heads):
    rph = rows // heads
    lane = lax.broadcasted_iota(jnp.int32, (rows, LANES), 1)
    shifts = [1 << s for s in range(LANES.bit_length() - 1)]
    qk_scale = ML_DIM ** -0.5

    def log_sigmoid(f):
        return jnp.minimum(f, 0.0) - jnp.log1p(jnp.exp(-jnp.abs(f)))

    def scan(y, combine, identity, reverse):
        for sh in shifts:
            if reverse:
                moved = jnp.where(lane < LANES - sh, pltpu.roll(y, LANES - sh, 1), identity)
            else:
                moved = jnp.where(lane >= sh, pltpu.roll(y, sh, 1), identity)
            y = combine(y, moved)
        return y

    for d, reverse in ((0, False), (1, True)):
        i_pre = g_ref[0, pl.ds((2 * d) * rows, rows), :]
        f_pre = g_ref[0, pl.ds((2 * d + 1) * rows, rows), :]
        b = scan(log_sigmoid(f_pre), jnp.add, 0.0, reverse)
        w = i_pre - b
        pm = scan(w, jnp.maximum, -jnp.inf, reverse)
        last = 0 if reverse else LANES - 1
        tot[0] = jnp.broadcast_to(b[:, last:last + 1], (rows, LANES))
        tot[1] = jnp.broadcast_to(pm[:, last:last + 1], (rows, LANES))

        def step(t, m):
            idx = pl.ds((rph - 1 - t) if reverse else t, heads, stride=rph)
            mst[idx, :] = m
            return tot[0, idx, :] + jnp.maximum(m, tot[1, idx, :])

        lax.fori_loop(0, rph, step, jnp.full((heads, LANES), -jnp.inf, F32))
        m_in = mst[...]
        mt = jnp.maximum(m_in, pm)
        mx = jnp.maximum(m_in, tot[1])
        base = ML_SCAN_KINDS * d
        o_ref[0, base + 0] = mt
        o_ref[0, base + 1] = jnp.exp(m_in - mt)
        o_ref[0, base + 2] = jnp.exp(-(b + mt))
        o_ref[0, base + 3] = jnp.exp(m_in - mx)
        o_ref[0, base + 4] = jnp.exp(w - mx) * qk_scale
        o_ref[0, base + 5] = w


def gate_scan(gates_t, rows, heads):
    b = gates_t.shape[0]
    kinds = 2 * ML_SCAN_KINDS
    return pl.pallas_call(
        functools.partial(_gate_scan_body, rows=rows, heads=heads),
        grid=(b,),
        in_specs=[pl.BlockSpec((1, 4 * rows, LANES), lambda i: (i, 0, 0))],
        out_specs=pl.BlockSpec((1, kinds, rows, LANES), lambda i: (i, 0, 0, 0)),
        out_shape=jax.ShapeDtypeStruct((b, kinds, rows, LANES), F32),
        scratch_shapes=[pltpu.VMEM((2, rows, LANES), F32), pltpu.VMEM((rows, LANES), F32)],
        compiler_params=_params("parallel"),
        name="gate_scan",
    )(gates_t)


ML_PARAMS = 16
ML_N_ROWS = 8
ML_NORM_ROWS = 256


def _mlstm_body(q_ref, k_ref, v_ref, mo_ref, par_ref, gain_ref, o_ref, hf, hb, cf, cb, nf, nb, *, seq):
    L = ML_BLOCK
    nblk = seq // L
    cf[...] = jnp.zeros_like(cf)
    cb[...] = jnp.zeros_like(cb)
    nf[...] = jnp.zeros_like(nf)
    nb[...] = jnp.zeros_like(nb)
    ti = lax.broadcasted_iota(jnp.int32, (L, L), 0)
    si = lax.broadcasted_iota(jnp.int32, (L, L), 1)

    def block(r0, base, c_ref, n_ref, h_ref, causal):
        qc = q_ref[0, pl.ds(r0, L), :]
        kc = k_ref[0, pl.ds(r0, L), :]
        vc = v_ref[0, pl.ds(r0, L), :]
        par = par_ref[0, 0, :, pl.ds(r0, L)]
        cols = par.T
        mt, w_inter, e_negm = (cols[:, base + i:base + i + 1] for i in range(3))
        decay = par[base + 3:base + 4, 0:1]
        ws_row = par[base + 4:base + 5, :]
        w_row = par[base + 5:base + 6, :]
        dmat = jnp.where(causal, jnp.exp(w_row - mt), 0.0)
        s = lax.dot_general(qc, kc, NT_DIMS, preferred_element_type=F32) * (ML_DIM ** -0.5)
        a = s * dmat
        intra = jnp.dot(a.astype(BF16), vc, preferred_element_type=F32)
        row_sum = jnp.sum(a, axis=1, keepdims=True)
        inter = jnp.dot(qc, c_ref[...].astype(BF16), preferred_element_type=F32)
        qn = lax.dot_general(qc, n_ref[...].astype(BF16), NT_DIMS, preferred_element_type=F32)[:, 0:1]
        num = w_inter * inter + intra
        den = w_inter * qn + row_sum
        h_ref[pl.ds(r0, L), :] = num / jnp.maximum(jnp.abs(den), e_negm)
        ws_rows = jnp.broadcast_to(ws_row, (ML_N_ROWS, L)).astype(BF16)
        n_ref[...] = decay * n_ref[...] + jnp.dot(ws_rows, kc, preferred_element_type=F32)
        ks_t = (kc.T.astype(F32) * ws_row).astype(BF16)
        for half in range(ML_DIM // LANES):
            rs = slice(half * LANES, (half + 1) * LANES)
            kv = jnp.dot(ks_t[rs, :], vc, preferred_element_type=F32)
            c_ref[rs, :] = decay * c_ref[rs, :] + kv

    def step(j, carry):
        block(pl.multiple_of(j * L, L), 0, cf, nf, hf, ti >= si)
        block(pl.multiple_of((nblk - 1 - j) * L, L), ML_SCAN_KINDS, cb, nb, hb, ti <= si)
        return carry

    lax.fori_loop(0, nblk, step, 0)

    rb = min(ML_NORM_ROWS, seq)

    def finish(i, carry):
        r0 = pl.multiple_of(i * rb, rb)
        h = hf[pl.ds(r0, rb), :] + hb[pl.ds(r0, rb), :]
        h = h * lax.rsqrt(jnp.mean(h * h, axis=-1, keepdims=True) + EPS)
        og = jax.nn.sigmoid(mo_ref[0, pl.ds(r0, rb), :].astype(F32))
        o_ref[0, pl.ds(r0, rb), :] = (h * gain_ref[...] * og).astype(o_ref.dtype)
        return carry

    lax.fori_loop(0, seq // rb, finish, 0)


def mlstm(proj, params, gain, seq):
    b = proj.shape[0]

    def pmap(off):
        return lambda bi, h: (bi, 0, off * ML_HEADS + h)

    blk = (1, seq, ML_DIM)
    return pl.pallas_call(
        functools.partial(_mlstm_body, seq=seq),
        grid=(b, ML_HEADS),
        in_specs=[
            pl.BlockSpec(blk, pmap(0)),
            pl.BlockSpec(blk, pmap(1)),
            pl.BlockSpec(blk, pmap(2)),
            pl.BlockSpec(blk, pmap(3)),
            pl.BlockSpec((1, 1, ML_PARAMS, seq), lambda bi, h: (bi, h, 0, 0)),
            pl.BlockSpec((1, ML_DIM), lambda bi, h: (0, h)),
        ],
        out_specs=pl.BlockSpec(blk, lambda bi, h: (bi, 0, h)),
        out_shape=jax.ShapeDtypeStruct((b, seq, ML_WIDTH), BF16),
        scratch_shapes=[
            pltpu.VMEM((seq, ML_DIM), F32),
            pltpu.VMEM((seq, ML_DIM), F32),
            pltpu.VMEM((ML_DIM, ML_DIM), F32),
            pltpu.VMEM((ML_DIM, ML_DIM), F32),
            pltpu.VMEM((ML_N_ROWS, ML_DIM), F32),
            pltpu.VMEM((ML_N_ROWS, ML_DIM), F32),
        ],
        compiler_params=_params("parallel", "parallel"),
        name="mlstm",
    )(proj, proj, proj, proj, params, gain.reshape(1, ML_WIDTH))


TOPK_PARTS = 16
TOPK_UNROLL = 4


def _topk_body(aff_ref, idx_ref, gate_ref, pos_s, *, n_experts, rows, cap):
    a3 = aff_ref[0]
    bits3 = pltpu.bitcast(a3, jnp.int32)

    def count(mask3):
        c = jnp.sum(mask3.astype(jnp.int32), axis=2, keepdims=True)
        return jnp.sum(c, axis=1, keepdims=True)

    def search(i, t):
        cand = t | jnp.left_shift(jnp.int32(1), 30 - i)
        return jnp.where(count(bits3 >= cand) >= cap, cand, t)

    thr = lax.fori_loop(0, 31, search, jnp.zeros((n_experts, 1, 1), jnp.int32))
    need = (cap - count(bits3 > thr)).astype(F32)

    er = n_experts * rows
    shape3 = (n_experts, rows, LANES)
    bits2 = bits3.reshape(er, LANES)
    thr2 = jnp.broadcast_to(thr, shape3).reshape(er, LANES)
    need2 = jnp.broadcast_to(need, shape3).reshape(er, LANES)
    ri = lax.broadcasted_iota(jnp.int32, (LANES, LANES), 0)
    ci = lax.broadcasted_iota(jnp.int32, (LANES, LANES), 1)
    upper = jnp.where(ri <= ci, 1.0, 0.0).astype(BF16)
    gi = lax.broadcasted_iota(jnp.int32, (er, er), 0)
    gj = lax.broadcasted_iota(jnp.int32, (er, er), 1)
    same_expert = (gi - gj) < (rows - (gj & (rows - 1)))
    earlier_rows = jnp.where((gj < gi) & same_expert, 1.0, 0.0).astype(BF16)

    def exclusive_prefix(mask_f):
        incl = jnp.dot(mask_f.astype(BF16), upper, preferred_element_type=F32)
        row_tot = jnp.broadcast_to(incl[:, LANES - 1:LANES], (er, LANES)).astype(BF16)
        before = jnp.dot(earlier_rows, row_tot, preferred_element_type=F32)
        return incl + before - mask_f

    eq_f = jnp.where(bits2 == thr2, 1.0, 0.0)
    tie_ok = (bits2 == thr2) & (exclusive_prefix(eq_f) < need2)
    sel = (bits2 > thr2) | tie_ok
    sel_f = jnp.where(sel, 1.0, 0.0)
    pos_s[...] = jnp.where(sel, exclusive_prefix(sel_f), -1.0).reshape(shape3)

    slot = lax.broadcasted_iota(jnp.int32, (cap, LANES), 0).astype(F32)
    lane_i = lax.broadcasted_iota(jnp.int32, (1, LANES), 1)
    part = lax.broadcasted_iota(jnp.int32, (TOPK_PARTS, LANES), 0)

    def per_expert(e, carry):
        def per_row(j, acc):
            hit = jnp.where(pos_s[e, pl.ds(j, 1), :] == slot, 1.0, 0.0).astype(BF16)
            a_row = aff_ref[0, e, pl.ds(j, 1), :]
            tok = lane_i + j * LANES
            a1 = a_row.astype(BF16).astype(F32)
            r1 = a_row - a1
            a2 = r1.astype(BF16).astype(F32)
            parts = [(tok >> 6).astype(F32), (tok & 63).astype(F32), a1, a2, r1 - a2]
            lhs = jnp.zeros((TOPK_PARTS, LANES), F32)
            for p, v in enumerate(parts):
                lhs = jnp.where(part == p, v, lhs)
            return acc + lax.dot_general(lhs.astype(BF16), hit, NT_DIMS, preferred_element_type=F32)

        acc = lax.fori_loop(0, rows, per_row, jnp.zeros((TOPK_PARTS, cap), F32), unroll=TOPK_UNROLL)
        idx_ref[0, e] = (acc[0:1] * 64.0 + acc[1:2]).astype(jnp.int32)
        gate_ref[0, e] = (acc[2:3] + acc[3:4]) + acc[4:5]
        return carry

    lax.fori_loop(0, n_experts, per_expert, 0)


def expert_topk(aff_t, cap):
    b, e, rows, _ = aff_t.shape
    assert rows & (rows - 1) == 0, "tokens per expert row group must be a power of two"
    return pl.pallas_call(
        functools.partial(_topk_body, n_experts=e, rows=rows, cap=cap),
        grid=(b,),
        in_specs=[pl.BlockSpec((1, e, rows, LANES), lambda i: (i, 0, 0, 0))],
        out_specs=[
            pl.BlockSpec((1, e, 1, cap), lambda i: (i, 0, 0, 0)),
            pl.BlockSpec((1, e, 1, cap), lambda i: (i, 0, 0, 0)),
        ],
        out_shape=[jax.ShapeDtypeStruct((b, e, 1, cap), jnp.int32), jax.ShapeDtypeStruct((b, e, 1, cap), F32)],
        scratch_shapes=[pltpu.VMEM((e, rows, LANES), F32)],
        compiler_params=_params("parallel"),
        name="expert_topk",
    )(aff_t)


DMA_UNROLL = 8
DOWN_COLS = 512


def _expert_ffn_body(idx_ref, prev_idx_ref, gate_ref, xn_hbm, x_hbm, wg_ref, wu_ref, wd_ref, o_hbm,
                     xe, xr, sems, *, seq, cap, n_steps):
    del x_hbm
    bsz = pl.num_programs(1)
    step = pl.program_id(0) * bsz + pl.program_id(1)
    slot = step % 2
    base = pl.program_id(1) * seq
    prev_base = ((pl.program_id(1) + bsz - 1) % bsz) * seq

    def xn_copy(c):
        row = base + idx_ref[0, 0, 0, c]
        return pltpu.make_async_copy(xn_hbm.at[pl.ds(row, 1), :], xe.at[pl.ds(c, 1), :], sems.at[0])

    def x_copy(c):
        row = base + idx_ref[0, 0, 0, c]
        return pltpu.make_async_copy(o_hbm.at[pl.ds(row, 1), :], xr.at[slot, pl.ds(c, 1), :], sems.at[1])

    def out_copy(c):
        row = base + idx_ref[0, 0, 0, c]
        return pltpu.make_async_copy(xr.at[slot, pl.ds(c, 1), :], o_hbm.at[pl.ds(row, 1), :], sems.at[2])

    def prev_out_copy(c):
        row = prev_base + prev_idx_ref[0, 0, 0, c]
        return pltpu.make_async_copy(xr.at[1 - slot, pl.ds(c, 1), :], o_hbm.at[pl.ds(row, 1), :], sems.at[2])

    def for_all(fn):
        lax.fori_loop(0, cap, lambda c, carry: (fn(c), carry)[1], 0, unroll=DMA_UNROLL)

    for_all(lambda c: (xn_copy(c).start(), x_copy(c).start()))

    @pl.when(step > 0)
    def _():
        for_all(lambda c: prev_out_copy(c).wait())

    for_all(lambda c: xn_copy(c).wait())
    x_in = xe[...].astype(BF16)
    hg = jnp.dot(x_in, wg_ref[0], preferred_element_type=F32)
    hu = jnp.dot(x_in, wu_ref[0], preferred_element_type=F32)
    hid = (jax.nn.silu(hg) * hu * gate_ref[0, 0]).astype(BF16)
    for_all(lambda c: x_copy(c).wait())
    d = xe.shape[1]
    dc = min(DOWN_COLS, d)
    for j in range(d // dc):
        cs = slice(j * dc, (j + 1) * dc)
        xr[slot, :, cs] += jnp.dot(hid, wd_ref[0, :, cs], preferred_element_type=F32)

    for_all(lambda c: out_copy(c).start())

    @pl.when(step == n_steps - 1)
    def _():
        for_all(lambda c: out_copy(c).wait())


def expert_ffn(idx, gate, xn, x, wg, wu, wd, layer, seq):
    bsz, n_e, cap, _ = gate.shape
    m, d = x.shape
    f = wg.shape[3]
    assert bsz >= 2, "consecutive grid steps must work on different batches"

    def prev_step(e, b):
        first = (e == 0) & (b == 0)
        pb = jnp.where(first, 0, (b + bsz - 1) % bsz)
        pe = jnp.where(first | (b > 0), e, e - 1)
        return pb, pe, 0, 0

    return pl.pallas_call(
        functools.partial(_expert_ffn_body, seq=seq, cap=cap, n_steps=n_e * bsz),
        grid=(n_e, bsz),
        in_specs=[
            pl.BlockSpec((1, 1, 1, cap), lambda e, b: (b, e, 0, 0), memory_space=pltpu.SMEM),
            pl.BlockSpec((1, 1, 1, cap), prev_step, memory_space=pltpu.SMEM),
            pl.BlockSpec((1, 1, cap, 1), lambda e, b: (b, e, 0, 0)),
            pl.BlockSpec(memory_space=pl.ANY),
            pl.BlockSpec(memory_space=pl.ANY),
            pl.BlockSpec((None, 1, d, f), lambda e, b: (layer, e, 0, 0)),
            pl.BlockSpec((None, 1, d, f), lambda e, b: (layer, e, 0, 0)),
            pl.BlockSpec((None, 1, f, d), lambda e, b: (layer, e, 0, 0)),
        ],
        out_specs=pl.BlockSpec(memory_space=pl.ANY),
        out_shape=jax.ShapeDtypeStruct((m, d), F32),
        scratch_shapes=[
            pltpu.VMEM((cap, d), F32),
            pltpu.VMEM((2, cap, d), F32),
            pltpu.SemaphoreType.DMA((3,)),
        ],
        input_output_aliases={4: 0},
        compiler_params=_params("arbitrary", "arbitrary"),
        name="expert_ffn",
    )(idx, idx, gate, xn, x, wg, wu, wd)


def _rope_tables(seq):
    inv_freq = ROPE_THETA ** (-jnp.arange(0, ROPE_DIM, 2, dtype=F32) / ROPE_DIM)
    ang = jnp.arange(seq, dtype=F32)[:, None] * inv_freq[None, :]
    cos, sin = jnp.cos(ang), jnp.sin(ang)
    rest = HEAD_DIM_A - ROPE_DIM
    cos_t = jnp.concatenate([cos, cos, jnp.ones((seq, rest), F32)], axis=1)
    sin_t = jnp.concatenate([-sin, sin, jnp.zeros((seq, rest), F32)], axis=1)
    scale = HEAD_DIM_A ** -0.5
    cos_tabs = jnp.stack([cos_t * scale, cos_t, jnp.ones_like(cos_t)])
    sin_tabs = jnp.stack([sin_t * scale, sin_t, jnp.zeros_like(sin_t)])
    return cos_tabs, sin_tabs


def _pad_cols(w, width):
    return jnp.pad(w, ((0, 0), (0, width - w.shape[1])))


def _mixer(x2d, xn, bsz, seq, layer, wt_in, gate_bias, ml_gain, w_attn, w_mlstm, w_out, rope_tabs):
    m = bsz * seq
    d_model = x2d.shape[1]
    qkv_w = 3 * ATTN_WIDTH

    qkv = qkv_rope_proj(xn, wt_in, layer, *rope_tabs, seq)
    ml_proj = matmul(xn, wt_in, layer, qkv_w, 4 * ML_WIDTH, BF16, tm=1024, tn=1024)
    gates = gates_proj(xn, wt_in, layer, MAIN_WIDTH, gate_bias)
    sig = matmul(xn, wt_in, layer, MAIN_WIDTH + N_GATES, 2 * d_model, BF16, tm=512, tn=1024, act="sigmoid")

    qkv3 = qkv.reshape(bsz, seq, qkv_w)
    outs, lses = [], []
    for g in range(len(ATTN_GROUPS)):
        o, lse = dilated_attention_group(qkv3, g, seq)
        outs.append(o)
        lses.append(lse)
    attn = attn_merge(outs, lses)

    rows = seq // LANES
    gates_t = gates.reshape(bsz, seq, N_GATES).transpose(0, 2, 1).reshape(bsz, N_GATES * rows, LANES)
    scans = gate_scan(gates_t, ML_HEADS * rows, ML_HEADS)
    params = scans.reshape(bsz, 2 * ML_SCAN_KINDS, ML_HEADS, seq).transpose(0, 2, 1, 3)
    params = jnp.pad(params, ((0, 0), (0, 0), (0, ML_PARAMS - 2 * ML_SCAN_KINDS), (0, 0)))
    hm = mlstm(ml_proj.reshape(bsz, seq, 4 * ML_WIDTH), params, ml_gain, seq).reshape(m, ML_WIDTH)

    merged = merge_branches(attn, hm, sig, w_attn, w_mlstm, layer)
    return out_proj_residual(merged, w_out, layer, x2d)


def _ffn(x2d, bsz, seq, layer, gain, w_router, w_gate, w_up, w_down):
    n_e = w_router.shape[1]
    cap = CAPACITY_FACTOR * seq // n_e
    xn, aff = rmsnorm_router(x2d, gain, _pad_cols(w_router, LANES).astype(BF16), n_e)
    aff_t = aff[:, :n_e].reshape(bsz, seq, n_e).transpose(0, 2, 1).reshape(bsz, n_e, seq // LANES, LANES)
    idx, gate = expert_topk(aff_t, cap)
    return expert_ffn(idx, gate.reshape(bsz, n_e, cap, 1), xn, x2d, w_gate, w_up, w_down, layer, seq)


def kernel(x, w_in, ml_gate_bias, ml_norm_gain, w_attn_branch, w_mlstm_branch, w_out, norm_mix_gain,
           norm_ffn_gain, w_router, w_expert_gate, w_expert_up, w_expert_down, final_norm_gain):
    bsz, seq, d_model = x.shape
    depth = w_in.shape[0]
    rope_tabs = _rope_tables(seq)
    wt_in = jnp.swapaxes(w_in, 1, 2)
    wg, wu, wd = (w.astype(BF16) for w in (w_expert_gate, w_expert_up, w_expert_down))
    x2d = x.reshape(bsz * seq, d_model)
    for layer in range(depth):
        xn = rmsnorm(x2d, norm_mix_gain[layer], BF16)
        x2d = _mixer(x2d, xn, bsz, seq, layer, wt_in, ml_gate_bias[layer], ml_norm_gain[layer],
                     w_attn_branch, w_mlstm_branch, w_out, rope_tabs)
        x2d = _ffn(x2d, bsz, seq, layer, norm_ffn_gain[layer], w_router[layer], wg, wu, wd)
    return rmsnorm(x2d, final_norm_gain, F32).reshape(bsz, seq, d_model)
```

```python
import functools

import jax
import jax.numpy as jnp
from jax import lax
from jax.experimental import pallas as pl
from jax.experimental.pallas import tpu as pltpu

F32 = jnp.float32
BF16 = jnp.bfloat16

ATTN_GROUPS = ((128, 1), (512, 4), (2048, 16))
HEADS_PER_GROUP = 4
HEAD_DIM_A = 128
N_ATTN_HEADS = len(ATTN_GROUPS) * HEADS_PER_GROUP
ATTN_WIDTH = N_ATTN_HEADS * HEAD_DIM_A
ATTN_OUT_WIDTH = HEADS_PER_GROUP * HEAD_DIM_A
ROPE_DIM = HEAD_DIM_A // 4
ROPE_HALF = ROPE_DIM // 2
ROPE_THETA = 500000.0
ML_HEADS = 8
ML_DIM = 256
ML_WIDTH = ML_HEADS * ML_DIM
ML_CHUNK = 64
GATE_SOFTCAP = 15.0
N_EXPERTS = 16
CAPACITY_FACTOR = 2
EPS = 1e-6
MAIN_WIDTH = 3 * ATTN_WIDTH + 4 * ML_WIDTH
N_GATES = 4 * ML_HEADS

LANES = 128
SUBLANES = 8
VMEM_LIMIT_BYTES = 60 * 1024 * 1024

NT_DIMS = (((1,), (1,)), ((), ()))
TN_DIMS = (((0,), (0,)), ((), ()))


def _params(*sem):
    return pltpu.CompilerParams(dimension_semantics=sem, vmem_limit_bytes=VMEM_LIMIT_BYTES)


def _rmsnorm_body(x_ref, g_ref, o_ref):
    x = x_ref[...]
    inv = lax.rsqrt(jnp.mean(x * x, axis=-1, keepdims=True) + EPS)
    o_ref[...] = (x * inv * g_ref[...]).astype(o_ref.dtype)


def rmsnorm(x2d, gain, out_dtype, tm=256):
    m, d = x2d.shape
    return pl.pallas_call(
        _rmsnorm_body,
        grid=(m // tm,),
        in_specs=[pl.BlockSpec((tm, d), lambda i: (i, 0)), pl.BlockSpec((1, d), lambda i: (0, 0))],
        out_specs=pl.BlockSpec((tm, d), lambda i: (i, 0)),
        out_shape=jax.ShapeDtypeStruct((m, d), out_dtype),
        compiler_params=_params("parallel"),
        name="rmsnorm",
    )(x2d, gain.reshape(1, d))


def _rmsnorm_router_body(x_ref, g_ref, wr_ref, xn_ref, aff_ref, *, n_experts):
    x = x_ref[...]
    inv = lax.rsqrt(jnp.mean(x * x, axis=-1, keepdims=True) + EPS)
    xn = x * inv * g_ref[...]
    xn_ref[...] = xn
    logits = jnp.dot(xn.astype(BF16), wr_ref[...], preferred_element_type=F32)
    lane = lax.broadcasted_iota(jnp.int32, logits.shape, 1)
    logits = jnp.where(lane < n_experts, logits, -jnp.inf)
    mx = jnp.max(logits, axis=-1, keepdims=True)
    p = jnp.exp(logits - mx)
    aff_ref[...] = p / jnp.sum(p, axis=-1, keepdims=True)


def rmsnorm_router(x2d, gain, w_router_pad, n_experts, tm=256):
    m, d = x2d.shape
    return pl.pallas_call(
        functools.partial(_rmsnorm_router_body, n_experts=n_experts),
        grid=(m // tm,),
        in_specs=[
            pl.BlockSpec((tm, d), lambda i: (i, 0)),
            pl.BlockSpec((1, d), lambda i: (0, 0)),
            pl.BlockSpec((d, LANES), lambda i: (0, 0)),
        ],
        out_specs=[pl.BlockSpec((tm, d), lambda i: (i, 0)), pl.BlockSpec((tm, LANES), lambda i: (i, 0))],
        out_shape=[jax.ShapeDtypeStruct((m, d), F32), jax.ShapeDtypeStruct((m, LANES), F32)],
        compiler_params=_params("parallel"),
        name="rmsnorm_router",
    )(x2d, gain.reshape(1, d), w_router_pad)


def _wt_spec(layer, row0, tn, k):
    assert row0 % SUBLANES == 0 and tn % SUBLANES == 0
    return pl.BlockSpec((pl.Element(1), pl.Element(tn), pl.Element(k)),
                        lambda j, i: (layer, pl.multiple_of(row0 + j * tn, SUBLANES), 0))


MM_CHUNK_ROWS = 512


def _row_chunks(tm):
    rows = min(MM_CHUNK_ROWS, tm)
    return [slice(r, r + rows) for r in range(0, tm, rows)]


def _matmul_body(a_ref, w_ref, o_ref, *, act):
    w = w_ref[0].astype(BF16)
    tm = a_ref.shape[0]
    for rs in (_row_chunks(tm) if act else [slice(0, tm)]):
        acc = lax.dot_general(a_ref[rs, :], w, NT_DIMS, preferred_element_type=F32)
        if act == "sigmoid":
            acc = jax.nn.sigmoid(acc)
        o_ref[rs, :] = acc.astype(o_ref.dtype)


def matmul(a, wt_stack, layer, row0, n, out_dtype, tm, tn, act=None):
    m, k = a.shape
    tm, tn = min(tm, m), min(tn, n)
    assert n % tn == 0
    return pl.pallas_call(
        functools.partial(_matmul_body, act=act),
        grid=(n // tn, m // tm),
        in_specs=[pl.BlockSpec((tm, k), lambda j, i: (i, 0)), _wt_spec(layer, row0, tn, k)],
        out_specs=pl.BlockSpec((tm, tn), lambda j, i: (i, j)),
        out_shape=jax.ShapeDtypeStruct((m, n), out_dtype),
        compiler_params=_params("parallel", "parallel"),
        name="proj_" + (act or "plain"),
    )(a, wt_stack)


def _gates_body(a_ref, w_ref, b_ref, o_ref):
    pre = lax.dot_general(a_ref[...], w_ref[0].astype(BF16), NT_DIMS, preferred_element_type=F32) + b_ref[...]
    o_ref[...] = GATE_SOFTCAP * jnp.tanh(pre / GATE_SOFTCAP)


def gates_proj(a, wt_stack, layer, row0, bias, tm=512):
    m, k = a.shape
    tm = min(tm, m)
    return pl.pallas_call(
        _gates_body,
        grid=(1, m // tm),
        in_specs=[
            pl.BlockSpec((tm, k), lambda j, i: (i, 0)),
            _wt_spec(layer, row0, N_GATES, k),
            pl.BlockSpec((1, N_GATES), lambda j, i: (0, 0)),
        ],
        out_specs=pl.BlockSpec((tm, N_GATES), lambda j, i: (i, 0)),
        out_shape=jax.ShapeDtypeStruct((m, N_GATES), F32),
        compiler_params=_params("parallel", "parallel"),
        name="gates_proj",
    )(a, wt_stack, bias.reshape(1, N_GATES))


def _merge_branches_body(at_ref, hm_ref, sga_ref, sgm_ref, wa_ref, wm_ref, o_ref):
    wa = wa_ref[...].astype(BF16)
    wm = wm_ref[...].astype(BF16)
    for rs in _row_chunks(at_ref.shape[0]):
        ya = jnp.dot(at_ref[rs, :], wa, preferred_element_type=F32)
        ym = jnp.dot(hm_ref[rs, :], wm, preferred_element_type=F32)
        o_ref[rs, :] = (sga_ref[rs, :].astype(F32) * ya + sgm_ref[rs, :].astype(F32) * ym).astype(o_ref.dtype)


def merge_branches(attn, hm, sig_gates, wa_stack, wm_stack, layer, tm=1024, tn=1024):
    m, ka = attn.shape
    km = hm.shape[1]
    n = wa_stack.shape[2]
    tm, tn = min(tm, m), min(tn, n)
    nb = n // tn
    return pl.pallas_call(
        _merge_branches_body,
        grid=(nb, m // tm),
        in_specs=[
            pl.BlockSpec((tm, ka), lambda j, i: (i, 0)),
            pl.BlockSpec((tm, km), lambda j, i: (i, 0)),
            pl.BlockSpec((tm, tn), lambda j, i: (i, j)),
            pl.BlockSpec((tm, tn), lambda j, i: (i, j + nb)),
            pl.BlockSpec((None, ka, tn), lambda j, i: (layer, 0, j)),
            pl.BlockSpec((None, km, tn), lambda j, i: (layer, 0, j)),
        ],
        out_specs=pl.BlockSpec((tm, tn), lambda j, i: (i, j)),
        out_shape=jax.ShapeDtypeStruct((m, n), BF16),
        compiler_params=_params("parallel", "parallel"),
        name="merge_branches",
    )(attn, hm, sig_gates, sig_gates, wa_stack, wm_stack)


def _out_proj_body(a_ref, w_ref, r_ref, o_ref):
    o_ref[...] = r_ref[...] + jnp.dot(a_ref[...], w_ref[...].astype(BF16), preferred_element_type=F32)


def out_proj_residual(a, w_stack, layer, resid, tm=512, tn=1024):
    m, k = a.shape
    n = w_stack.shape[2]
    tm, tn = min(tm, m), min(tn, n)
    return pl.pallas_call(
        _out_proj_body,
        grid=(n // tn, m // tm),
        in_specs=[
            pl.BlockSpec((tm, k), lambda j, i: (i, 0)),
            pl.BlockSpec((None, k, tn), lambda j, i: (layer, 0, j)),
            pl.BlockSpec((tm, tn), lambda j, i: (i, j)),
        ],
        out_specs=pl.BlockSpec((tm, tn), lambda j, i: (i, j)),
        out_shape=jax.ShapeDtypeStruct((m, n), F32),
        compiler_params=_params("parallel", "parallel"),
        name="out_proj_residual",
    )(a, w_stack, resid)


def _qkv_rope_body(a_ref, w_ref, cos_ref, sin_ref, o_ref):
    w = w_ref[0].astype(BF16)
    tn = w.shape[0]
    for rs in _row_chunks(a_ref.shape[0]):
        acc = lax.dot_general(a_ref[rs, :], w, NT_DIMS, preferred_element_type=F32)
        c = cos_ref[0, rs, :]
        s = sin_ref[0, rs, :]
        first_half = lax.broadcasted_iota(jnp.int32, c.shape, 1) < ROPE_HALF
        for h in range(tn // HEAD_DIM_A):
            hs = slice(h * HEAD_DIM_A, (h + 1) * HEAD_DIM_A)
            t = acc[:, hs]
            partner = jnp.where(first_half, pltpu.roll(t, LANES - ROPE_HALF, 1), pltpu.roll(t, ROPE_HALF, 1))
            o_ref[rs, hs] = t * c + partner * s


QKV_TILES_PER_ROLE = 2


def qkv_rope_proj(a, wt_stack, layer, cos_tabs, sin_tabs, seq, tm=1024):
    m, k = a.shape
    n = 3 * ATTN_WIDTH
    tpr = QKV_TILES_PER_ROLE
    tn = ATTN_WIDTH // tpr
    tm = min(tm, seq)
    per_seq = seq // tm
    return pl.pallas_call(
        _qkv_rope_body,
        grid=(3 * tpr, m // tm),
        in_specs=[
            pl.BlockSpec((tm, k), lambda j, i: (i, 0)),
            _wt_spec(layer, 0, tn, k),
            pl.BlockSpec((1, tm, LANES), lambda j, i: (j // tpr, i % per_seq, 0)),
            pl.BlockSpec((1, tm, LANES), lambda j, i: (j // tpr, i % per_seq, 0)),
        ],
        out_specs=pl.BlockSpec((tm, tn), lambda j, i: (i, j)),
        out_shape=jax.ShapeDtypeStruct((m, n), F32),
        compiler_params=_params("parallel", "parallel"),
        name="qkv_rope_proj",
    )(a, wt_stack, cos_tabs, sin_tabs)


ATTN_TQ = 128
ATTN_LOAD_ROWS = 256
ATTN_UNROLL = 4
ATTN_RESIDUES = 2


def _attn_body(q_ref, k_ref, v_ref, o_ref, lse_ref, qs2, ks2, vs2, *, seq, dil, radius):
    n = seq // dil
    tq = ATTN_TQ
    kw = tq + 2 * radius
    rb = min(ATTN_LOAD_ROWS, n)
    row = lax.broadcasted_iota(jnp.int32, (tq, kw), 0)
    col = lax.broadcasted_iota(jnp.int32, (tq, kw), 1)

    def residue(r, buf):
        qs, ks, vs = qs2.at[buf], ks2.at[buf], vs2.at[buf]

        def regroup(i, c):
            a0 = pl.multiple_of(i * rb, rb)
            src = pl.ds(r + a0 * dil, rb, stride=dil)
            qs[pl.ds(a0, rb), :] = q_ref[0, src, :].astype(BF16)
            ks[pl.ds(a0, rb), :] = k_ref[0, src, :].astype(BF16)
            vs[pl.ds(a0, rb), :] = v_ref[0, src, :].astype(BF16)
            return c

        lax.fori_loop(0, n // rb, regroup, 0)

        def q_block(i, c):
            q0 = pl.multiple_of(i * tq, tq)
            ws = pl.multiple_of(jnp.clip(q0 - radius, 0, n - kw), radius)
            qb = qs[pl.ds(q0, tq), :]
            kb = ks[pl.ds(ws, kw), :]
            vb = vs[pl.ds(ws, kw), :]
            s = lax.dot_general(qb, kb, NT_DIMS, preferred_element_type=F32)
            dist = (col + ws) - (row + q0)
            s = jnp.where(jnp.abs(dist) <= radius, s, -jnp.inf)
            mx = jnp.max(s, axis=-1, keepdims=True)
            p = jnp.exp(s - mx)
            l = jnp.sum(p, axis=-1, keepdims=True)
            o = jnp.dot(p.astype(BF16), vb, preferred_element_type=F32) / l
            dst = pl.ds(r + q0 * dil, tq, stride=dil)
            o_ref[0, dst, :] = o
            lse_ref[0, dst, :] = jnp.broadcast_to(mx + jnp.log(l), (tq, LANES))
            return c

        lax.fori_loop(0, n // tq, q_block, 0, unroll=ATTN_UNROLL)

    ways = min(ATTN_RESIDUES, dil)

    def residues(i, carry):
        for w in range(ways):
            residue(i * ways + w, w)
        return carry

    lax.fori_loop(0, dil // ways, residues, 0)


def dilated_attention_group(qkv, group, seq):
    window, dil = ATTN_GROUPS[group]
    radius = window // (2 * dil)
    b = qkv.shape[0]
    n = seq // dil
    hg = HEADS_PER_GROUP
    head0 = group * hg

    def qmap(off):
        return lambda bi, h: (bi, 0, off + head0 + h)

    blk = (1, seq, LANES)
    out_sd = jax.ShapeDtypeStruct((b, seq, hg * LANES), F32)
    o, lse = pl.pallas_call(
        functools.partial(_attn_body, seq=seq, dil=dil, radius=radius),
        grid=(b, hg),
        in_specs=[
            pl.BlockSpec(blk, qmap(0)),
            pl.BlockSpec(blk, qmap(N_ATTN_HEADS)),
            pl.BlockSpec(blk, qmap(2 * N_ATTN_HEADS)),
        ],
        out_specs=[pl.BlockSpec(blk, lambda bi, h: (bi, 0, h)), pl.BlockSpec(blk, lambda bi, h: (bi, 0, h))],
        out_shape=[out_sd, out_sd],
        scratch_shapes=[pltpu.VMEM((ATTN_RESIDUES, n, LANES), BF16)] * 3,
        compiler_params=_params("parallel", "parallel"),
        name=f"dilated_attn_g{group}",
    )(qkv, qkv, qkv)
    width = hg * LANES
    return o.reshape(b * seq, width), lse.reshape(b * seq, width)


def _attn_merge_body(o0, o1, o2, l0, l1, l2, out_ref):
    la, lb, lc = l0[...], l1[...], l2[...]
    mx = jnp.maximum(jnp.maximum(la, lb), lc)
    wa, wb, wc = jnp.exp(la - mx), jnp.exp(lb - mx), jnp.exp(lc - mx)
    acc = wa * o0[...] + wb * o1[...] + wc * o2[...]
    out_ref[...] = (acc / (wa + wb + wc)).astype(out_ref.dtype)


def attn_merge(outs, lses, tm=1024):
    m, w = outs[0].shape
    spec = pl.BlockSpec((tm, w), lambda i: (i, 0))
    return pl.pallas_call(
        _attn_merge_body,
        grid=(m // tm,),
        in_specs=[spec] * 6,
        out_specs=spec,
        out_shape=jax.ShapeDtypeStruct((m, w), BF16),
        compiler_params=_params("parallel"),
        name="attn_merge",
    )(*outs, *lses)


ML_SCAN_KINDS = 6
ML_BLOCK = LANES


def _gate_scan_body(g_ref, o_ref, tot, mst, *, rows, heads):
    rph = rows // heads
    lane = lax.broadcasted_iota(jnp.int32, (rows, LANES), 1)
    shifts = [1 << s for s in range(LANES.bit_length() - 1)]
    qk_scale = ML_DIM ** -0.5

    def log_sigmoid(f):
        return jnp.minimum(f, 0.0) - jnp.log1p(jnp.exp(-jnp.abs(f)))

    def scan(y, combine, identity, reverse):
        for sh in shifts:
            if reverse:
                moved = jnp.where(lane < LANES - sh, pltpu.roll(y, LANES - sh, 1), identity)
            else:
                moved = jnp.where(lane >= sh, pltpu.roll(y, sh, 1), identity)
            y = combine(y, moved)
        return y

    for d, reverse in ((0, False), (1, True)):
        i_pre = g_ref[0, pl.ds((2 * d) * rows, rows), :]
        f_pre = g_ref[0, pl.ds((2 * d + 1) * rows, rows), :]
        b = scan(log_sigmoid(f_pre), jnp.add, 0.0, reverse)
        w = i_pre - b
        pm = scan(w, jnp.maximum, -jnp.inf, reverse)
        last = 0 if reverse else LANES - 1
        tot[0] = jnp.broadcast_to(b[:, last:last + 1], (rows, LANES))
        tot[1] = jnp.broadcast_to(pm[:, last:last + 1], (rows, LANES))

        def step(t, m):
            idx = pl.ds((rph - 1 - t) if reverse else t, heads, stride=rph)
            mst[idx, :] = m
            return tot[0, idx, :] + jnp.maximum(m, tot[1, idx, :])

        lax.fori_loop(0, rph, step, jnp.full((heads, LANES), -jnp.inf, F32))
        m_in = mst[...]
        mt = jnp.maximum(m_in, pm)
        mx = jnp.maximum(m_in, tot[1])
        base = ML_SCAN_KINDS * d
        o_ref[0, base + 0] = mt
        o_ref[0, base + 1] = jnp.exp(m_in - mt)
        o_ref[0, base + 2] = jnp.exp(-(b + mt))
        o_ref[0, base + 3] = jnp.exp(m_in - mx)
        o_ref[0, base + 4] = jnp.exp(w - mx) * qk_scale
        o_ref[0, base + 5] = w


def gate_scan(gates_t, rows, heads):
    b = gates_t.shape[0]
    kinds = 2 * ML_SCAN_KINDS
    return pl.pallas_call(
        functools.partial(_gate_scan_body, rows=rows, heads=heads),
        grid=(b,),
        in_specs=[pl.BlockSpec((1, 4 * rows, LANES), lambda i: (i, 0, 0))],
        out_specs=pl.BlockSpec((1, kinds, rows, LANES), lambda i: (i, 0, 0, 0)),
        out_shape=jax.ShapeDtypeStruct((b, kinds, rows, LANES), F32),
        scratch_shapes=[pltpu.VMEM((2, rows, LANES), F32), pltpu.VMEM((rows, LANES), F32)],
        compiler_params=_params("parallel"),
        name="gate_scan",
    )(gates_t)


ML_PARAMS = 16
ML_N_ROWS = 8
ML_NORM_ROWS = 256


def _mlstm_body(q_ref, k_ref, v_ref, mo_ref, par_ref, gain_ref, o_ref, hf, hb, cf, cb, nf, nb, *, seq):
    L = ML_BLOCK
    nblk = seq // L
    cf[...] = jnp.zeros_like(cf)
    cb[...] = jnp.zeros_like(cb)
    nf[...] = jnp.zeros_like(nf)
    nb[...] = jnp.zeros_like(nb)
    ti = lax.broadcasted_iota(jnp.int32, (L, L), 0)
    si = lax.broadcasted_iota(jnp.int32, (L, L), 1)

    def block(r0, base, c_ref, n_ref, h_ref, causal):
        qc = q_ref[0, pl.ds(r0, L), :]
        kc = k_ref[0, pl.ds(r0, L), :]
        vc = v_ref[0, pl.ds(r0, L), :]
        par = par_ref[0, 0, :, pl.ds(r0, L)]
        cols = par.T
        mt, w_inter, e_negm = (cols[:, base + i:base + i + 1] for i in range(3))
        decay = par[base + 3:base + 4, 0:1]
        ws_row = par[base + 4:base + 5, :]
        w_row = par[base + 5:base + 6, :]
        dmat = jnp.where(causal, jnp.exp(w_row - mt), 0.0)
        s = lax.dot_general(qc, kc, NT_DIMS, preferred_element_type=F32) * (ML_DIM ** -0.5)
        a = s * dmat
        intra = jnp.dot(a.astype(BF16), vc, preferred_element_type=F32)
        row_sum = jnp.sum(a, axis=1, keepdims=True)
        inter = jnp.dot(qc, c_ref[...].astype(BF16), preferred_element_type=F32)
        qn = lax.dot_general(qc, n_ref[...].astype(BF16), NT_DIMS, preferred_element_type=F32)[:, 0:1]
        num = w_inter * inter + intra
        den = w_inter * qn + row_sum
        h_ref[pl.ds(r0, L), :] = num / jnp.maximum(jnp.abs(den), e_negm)
        ws_rows = jnp.broadcast_to(ws_row, (ML_N_ROWS, L)).astype(BF16)
        n_ref[...] = decay * n_ref[...] + jnp.dot(ws_rows, kc, preferred_element_type=F32)
        ks_t = (kc.T.astype(F32) * ws_row).astype(BF16)
        for half in range(ML_DIM // LANES):
            rs = slice(half * LANES, (half + 1) * LANES)
            kv = jnp.dot(ks_t[rs, :], vc, preferred_element_type=F32)
            c_ref[rs, :] = decay * c_ref[rs, :] + kv

    def step(j, carry):
        block(pl.multiple_of(j * L, L), 0, cf, nf, hf, ti >= si)
        block(pl.multiple_of((nblk - 1 - j) * L, L), ML_SCAN_KINDS, cb, nb, hb, ti <= si)
        return carry

    lax.fori_loop(0, nblk, step, 0)

    rb = min(ML_NORM_ROWS, seq)

    def finish(i, carry):
        r0 = pl.multiple_of(i * rb, rb)
        h = hf[pl.ds(r0, rb), :] + hb[pl.ds(r0, rb), :]
        h = h * lax.rsqrt(jnp.mean(h * h, axis=-1, keepdims=True) + EPS)
        og = jax.nn.sigmoid(mo_ref[0, pl.ds(r0, rb), :].astype(F32))
        o_ref[0, pl.ds(r0, rb), :] = (h * gain_ref[...] * og).astype(o_ref.dtype)
        return carry

    lax.fori_loop(0, seq // rb, finish, 0)


def mlstm(proj, params, gain, seq):
    b = proj.shape[0]

    def pmap(off):
        return lambda bi, h: (bi, 0, off * ML_HEADS + h)

    blk = (1, seq, ML_DIM)
    return pl.pallas_call(
        functools.partial(_mlstm_body, seq=seq),
        grid=(b, ML_HEADS),
        in_specs=[
            pl.BlockSpec(blk, pmap(0)),
            pl.BlockSpec(blk, pmap(1)),
            pl.BlockSpec(blk, pmap(2)),
            pl.BlockSpec(blk, pmap(3)),
            pl.BlockSpec((1, 1, ML_PARAMS, seq), lambda bi, h: (bi, h, 0, 0)),
            pl.BlockSpec((1, ML_DIM), lambda bi, h: (0, h)),
        ],
        out_specs=pl.BlockSpec(blk, lambda bi, h: (bi, 0, h)),
        out_shape=jax.ShapeDtypeStruct((b, seq, ML_WIDTH), BF16),
        scratch_shapes=[
            pltpu.VMEM((seq, ML_DIM), F32),
            pltpu.VMEM((seq, ML_DIM), F32),
            pltpu.VMEM((ML_DIM, ML_DIM), F32),
            pltpu.VMEM((ML_DIM, ML_DIM), F32),
            pltpu.VMEM((ML_N_ROWS, ML_DIM), F32),
            pltpu.VMEM((ML_N_ROWS, ML_DIM), F32),
        ],
        compiler_params=_params("parallel", "parallel"),
        name="mlstm",
    )(proj, proj, proj, proj, params, gain.reshape(1, ML_WIDTH))


TOPK_PARTS = 16
TOPK_UNROLL = 4


def _topk_body(aff_ref, idx_ref, gate_ref, pos_s, *, n_experts, rows, cap):
    a3 = aff_ref[0]
    bits3 = pltpu.bitcast(a3, jnp.int32)

    def count(mask3):
        c = jnp.sum(mask3.astype(jnp.int32), axis=2, keepdims=True)
        return jnp.sum(c, axis=1, keepdims=True)

    def search(i, t):
        cand = t | jnp.left_shift(jnp.int32(1), 30 - i)
        return jnp.where(count(bits3 >= cand) >= cap, cand, t)

    thr = lax.fori_loop(0, 31, search, jnp.zeros((n_experts, 1, 1), jnp.int32))
    need = (cap - count(bits3 > thr)).astype(F32)

    er = n_experts * rows
    shape3 = (n_experts, rows, LANES)
    bits2 = bits3.reshape(er, LANES)
    thr2 = jnp.broadcast_to(thr, shape3).reshape(er, LANES)
    need2 = jnp.broadcast_to(need, shape3).reshape(er, LANES)
    ri = lax.broadcasted_iota(jnp.int32, (LANES, LANES), 0)
    ci = lax.broadcasted_iota(jnp.int32, (LANES, LANES), 1)
    upper = jnp.where(ri <= ci, 1.0, 0.0).astype(BF16)
    gi = lax.broadcasted_iota(jnp.int32, (er, er), 0)
    gj = lax.broadcasted_iota(jnp.int32, (er, er), 1)
    same_expert = (gi - gj) < (rows - (gj & (rows - 1)))
    earlier_rows = jnp.where((gj < gi) & same_expert, 1.0, 0.0).astype(BF16)

    def exclusive_prefix(mask_f):
        incl = jnp.dot(mask_f.astype(BF16), upper, preferred_element_type=F32)
        row_tot = jnp.broadcast_to(incl[:, LANES - 1:LANES], (er, LANES)).astype(BF16)
        before = jnp.dot(earlier_rows, row_tot, preferred_element_type=F32)
        return incl + before - mask_f

    eq_f = jnp.where(bits2 == thr2, 1.0, 0.0)
    tie_ok = (bits2 == thr2) & (exclusive_prefix(eq_f) < need2)
    sel = (bits2 > thr2) | tie_ok
    sel_f = jnp.where(sel, 1.0, 0.0)
    pos_s[...] = jnp.where(sel, exclusive_prefix(sel_f), -1.0).reshape(shape3)

    slot = lax.broadcasted_iota(jnp.int32, (cap, LANES), 0).astype(F32)
    lane_i = lax.broadcasted_iota(jnp.int32, (1, LANES), 1)
    part = lax.broadcasted_iota(jnp.int32, (TOPK_PARTS, LANES), 0)

    def per_expert(e, carry):
        def per_row(j, acc):
            hit = jnp.where(pos_s[e, pl.ds(j, 1), :] == slot, 1.0, 0.0).astype(BF16)
            a_row = aff_ref[0, e, pl.ds(j, 1), :]
            tok = lane_i + j * LANES
            a1 = a_row.astype(BF16).astype(F32)
            r1 = a_row - a1
            a2 = r1.astype(BF16).astype(F32)
            parts = [(tok >> 6).astype(F32), (tok & 63).astype(F32), a1, a2, r1 - a2]
            lhs = jnp.zeros((TOPK_PARTS, LANES), F32)
            for p, v in enumerate(parts):
                lhs = jnp.where(part == p, v, lhs)
            return acc + lax.dot_general(lhs.astype(BF16), hit, NT_DIMS, preferred_element_type=F32)

        acc = lax.fori_loop(0, rows, per_row, jnp.zeros((TOPK_PARTS, cap), F32), unroll=TOPK_UNROLL)
        idx_ref[0, e] = (acc[0:1] * 64.0 + acc[1:2]).astype(jnp.int32)
        gate_ref[0, e] = (acc[2:3] + acc[3:4]) + acc[4:5]
        return carry

    lax.fori_loop(0, n_experts, per_expert, 0)


def expert_topk(aff_t, cap):
    b, e, rows, _ = aff_t.shape
    assert rows & (rows - 1) == 0, "tokens per expert row group must be a power of two"
    return pl.pallas_call(
        functools.partial(_topk_body, n_experts=e, rows=rows, cap=cap),
        grid=(b,),
        in_specs=[pl.BlockSpec((1, e, rows, LANES), lambda i: (i, 0, 0, 0))],
        out_specs=[
            pl.BlockSpec((1, e, 1, cap), lambda i: (i, 0, 0, 0)),
            pl.BlockSpec((1, e, 1, cap), lambda i: (i, 0, 0, 0)),
        ],
        out_shape=[jax.ShapeDtypeStruct((b, e, 1, cap), jnp.int32), jax.ShapeDtypeStruct((b, e, 1, cap), F32)],
        scratch_shapes=[pltpu.VMEM((e, rows, LANES), F32)],
        compiler_params=_params("parallel"),
        name="expert_topk",
    )(aff_t)


DMA_UNROLL = 8
DOWN_COLS = 512


def _expert_ffn_body(idx_ref, prev_idx_ref, gate_ref, xn_hbm, x_hbm, wg_ref, wu_ref, wd_ref, o_hbm,
                     xe, xr, sems, *, seq, cap, n_steps):
    del x_hbm
    bsz = pl.num_programs(1)
    step = pl.program_id(0) * bsz + pl.program_id(1)
    slot = step % 2
    base = pl.program_id(1) * seq
    prev_base = ((pl.program_id(1) + bsz - 1) % bsz) * seq

    def xn_copy(c):
        row = base + idx_ref[0, 0, 0, c]
        return pltpu.make_async_copy(xn_hbm.at[pl.ds(row, 1), :], xe.at[pl.ds(c, 1), :], sems.at[0])

    def x_copy(c):
        row = base + idx_ref[0, 0, 0, c]
        return pltpu.make_async_copy(o_hbm.at[pl.ds(row, 1), :], xr.at[slot, pl.ds(c, 1), :], sems.at[1])

    def out_copy(c):
        row = base + idx_ref[0, 0, 0, c]
        return pltpu.make_async_copy(xr.at[slot, pl.ds(c, 1), :], o_hbm.at[pl.ds(row, 1), :], sems.at[2])

    def prev_out_copy(c):
        row = prev_base + prev_idx_ref[0, 0, 0, c]
        return pltpu.make_async_copy(xr.at[1 - slot, pl.ds(c, 1), :], o_hbm.at[pl.ds(row, 1), :], sems.at[2])

    def for_all(fn):
        lax.fori_loop(0, cap, lambda c, carry: (fn(c), carry)[1], 0, unroll=DMA_UNROLL)

    for_all(lambda c: (xn_copy(c).start(), x_copy(c).start()))

    @pl.when(step > 0)
    def _():
        for_all(lambda c: prev_out_copy(c).wait())

    for_all(lambda c: xn_copy(c).wait())
    x_in = xe[...].astype(BF16)
    hg = jnp.dot(x_in, wg_ref[0], preferred_element_type=F32)
    hu = jnp.dot(x_in, wu_ref[0], preferred_element_type=F32)
    hid = (jax.nn.silu(hg) * hu * gate_ref[0, 0]).astype(BF16)
    for_all(lambda c: x_copy(c).wait())
    d = xe.shape[1]
    dc = min(DOWN_COLS, d)
    for j in range(d // dc):
        cs = slice(j * dc, (j + 1) * dc)
        xr[slot, :, cs] += jnp.dot(hid, wd_ref[0, :, cs], preferred_element_type=F32)

    for_all(lambda c: out_copy(c).start())

    @pl.when(step == n_steps - 1)
    def _():
        for_all(lambda c: out_copy(c).wait())


def expert_ffn(idx, gate, xn, x, wg, wu, wd, layer, seq):
    bsz, n_e, cap, _ = gate.shape
    m, d = x.shape
    f = wg.shape[3]
    assert bsz >= 2, "consecutive grid steps must work on different batches"

    def prev_step(e, b):
        first = (e == 0) & (b == 0)
        pb = jnp.where(first, 0, (b + bsz - 1) % bsz)
        pe = jnp.where(first | (b > 0), e, e - 1)
        return pb, pe, 0, 0

    return pl.pallas_call(
        functools.partial(_expert_ffn_body, seq=seq, cap=cap, n_steps=n_e * bsz),
        grid=(n_e, bsz),
        in_specs=[
            pl.BlockSpec((1, 1, 1, cap), lambda e, b: (b, e, 0, 0), memory_space=pltpu.SMEM),
            pl.BlockSpec((1, 1, 1, cap), prev_step, memory_space=pltpu.SMEM),
            pl.BlockSpec((1, 1, cap, 1), lambda e, b: (b, e, 0, 0)),
            pl.BlockSpec(memory_space=pl.ANY),
            pl.BlockSpec(memory_space=pl.ANY),
            pl.BlockSpec((None, 1, d, f), lambda e, b: (layer, e, 0, 0)),
            pl.BlockSpec((None, 1, d, f), lambda e, b: (layer, e, 0, 0)),
            pl.BlockSpec((None, 1, f, d), lambda e, b: (layer, e, 0, 0)),
        ],
        out_specs=pl.BlockSpec(memory_space=pl.ANY),
        out_shape=jax.ShapeDtypeStruct((m, d), F32),
        scratch_shapes=[
            pltpu.VMEM((cap, d), F32),
            pltpu.VMEM((2, cap, d), F32),
            pltpu.SemaphoreType.DMA((3,)),
        ],
        input_output_aliases={4: 0},
        compiler_params=_params("arbitrary", "arbitrary"),
        name="expert_ffn",
    )(idx, idx, gate, xn, x, wg, wu, wd)


def _rope_tables(seq):
    inv_freq = ROPE_THETA ** (-jnp.arange(0, ROPE_DIM, 2, dtype=F32) / ROPE_DIM)
    ang = jnp.arange(seq, dtype=F32)[:, None] * inv_freq[None, :]
    cos, sin = jnp.cos(ang), jnp.sin(ang)
    rest = HEAD_DIM_A - ROPE_DIM
    cos_t = jnp.concatenate([cos, cos, jnp.ones((seq, rest), F32)], axis=1)
    sin_t = jnp.concatenate([-sin, sin, jnp.zeros((seq, rest), F32)], axis=1)
    scale = HEAD_DIM_A ** -0.5
    cos_tabs = jnp.stack([cos_t * scale, cos_t, jnp.ones_like(cos_t)])
    sin_tabs = jnp.stack([sin_t * scale, sin_t, jnp.zeros_like(sin_t)])
    return cos_tabs, sin_tabs


def _pad_cols(w, width):
    return jnp.pad(w, ((0, 0), (0, width - w.shape[1])))


def _mixer(x2d, xn, bsz, seq, layer, wt_in, gate_bias, ml_gain, w_attn, w_mlstm, w_out, rope_tabs):
    m = bsz * seq
    d_model = x2d.shape[1]
    qkv_w = 3 * ATTN_WIDTH

    qkv = qkv_rope_proj(xn, wt_in, layer, *rope_tabs, seq)
    ml_proj = matmul(xn, wt_in, layer, qkv_w, 4 * ML_WIDTH, BF16, tm=1024, tn=1024)
    gates = gates_proj(xn, wt_in, layer, MAIN_WIDTH, gate_bias)
    sig = matmul(xn, wt_in, layer, MAIN_WIDTH + N_GATES, 2 * d_model, BF16, tm=1024, tn=512, act="sigmoid")

    qkv3 = qkv.reshape(bsz, seq, qkv_w)
    outs, lses = [], []
    for g in range(len(ATTN_GROUPS)):
        o, lse = dilated_attention_group(qkv3, g, seq)
        outs.append(o)
        lses.append(lse)
    attn = attn_merge(outs, lses)

    rows = seq // LANES
    gates_t = gates.reshape(bsz, seq, N_GATES).transpose(0, 2, 1).reshape(bsz, N_GATES * rows, LANES)
    scans = gate_scan(gates_t, ML_HEADS * rows, ML_HEADS)
    params = scans.reshape(bsz, 2 * ML_SCAN_KINDS, ML_HEADS, seq).transpose(0, 2, 1, 3)
    params = jnp.pad(params, ((0, 0), (0, 0), (0, ML_PARAMS - 2 * ML_SCAN_KINDS), (0, 0)))
    hm = mlstm(ml_proj.reshape(bsz, seq, 4 * ML_WIDTH), params, ml_gain, seq).reshape(m, ML_WIDTH)

    merged = merge_branches(attn, hm, sig, w_attn, w_mlstm, layer)
    return out_proj_residual(merged, w_out, layer, x2d)


def _ffn(x2d, bsz, seq, layer, gain, w_router, w_gate, w_up, w_down):
    n_e = w_router.shape[1]
    cap = CAPACITY_FACTOR * seq // n_e
    xn, aff = rmsnorm_router(x2d, gain, _pad_cols(w_router, LANES).astype(BF16), n_e)
    aff_t = aff[:, :n_e].reshape(bsz, seq, n_e).transpose(0, 2, 1).reshape(bsz, n_e, seq // LANES, LANES)
    idx, gate = expert_topk(aff_t, cap)
    return expert_ffn(idx, gate.reshape(bsz, n_e, cap, 1), xn, x2d, w_gate, w_up, w_down, layer, seq)


def kernel(x, w_in, ml_gate_bias, ml_norm_gain, w_attn_branch, w_mlstm_branch, w_out, norm_mix_gain,
           norm_ffn_gain, w_router, w_expert_gate, w_expert_up, w_expert_down, final_norm_gain):
    bsz, seq, d_model = x.shape
    depth = w_in.shape[0]
    rope_tabs = _rope_tables(seq)
    wt_in = jnp.swapaxes(w_in, 1, 2)
    wg, wu, wd = (w.astype(BF16) for w in (w_expert_gate, w_expert_up, w_expert_down))
    x2d = x.reshape(bsz * seq, d_model)
    for layer in range(depth):
        xn = rmsnorm(x2d, norm_mix_gain[layer], BF16)
        x2d = _mixer(x2d, xn, bsz, seq, layer, wt_in, ml_gate_bias[layer], ml_norm_gain[layer],
                     w_attn_branch, w_mlstm_branch, w_out, rope_tabs)
        x2d = _ffn(x2d, bsz, seq, layer, norm_ffn_gain[layer], w_router[layer], wg, wu, wd)
    return rmsnorm(x2d, final_norm_gain, F32).reshape(bsz, seq, d_model)
```
